```python
import jax, jax.numpy as jnp
from jax import lax
import numpy as np

D_MODEL = 4096
BATCH = 1
SEQ = 8192
DEPTH = 1

CTX_LEN = 256
GRID_W = 64
MIX_WIDTH = D_MODEL
A_WIDTH = MIX_WIDTH // 2
A_GROUPS = 16
A_GROUP_DIM = A_WIDTH // A_GROUPS
CHUNK = 128
B_WIDTH = MIX_WIDTH - A_WIDTH
HEAD_DIM = 128
N_HEADS = B_WIDTH // HEAD_DIM
N_KV_HEADS = 4
KV_GROUP = N_HEADS // N_KV_HEADS
KV_WIDTH = N_KV_HEADS * HEAD_DIM
Q_BLOCK = 128
ROPE_THETA = 10000.0
D_FF = 11008
CONV_W = 3
EPS = 1e-6
N_MOD = 6
IN_COLS = 2 * A_WIDTH + B_WIDTH + 2 * KV_WIDTH
SPLITS = (2 * A_WIDTH, 2 * A_WIDTH + B_WIDTH, 2 * A_WIDTH + B_WIDTH + KV_WIDTH)

kernel_name = 'hybrid_gmlp_gqa_dit_block'


def rms_norm(x, g):
    xf = x.astype(jnp.float32)
    y = xf * lax.rsqrt(jnp.mean(xf * xf, axis=-1, keepdims=True) + EPS)
    return (y * g.astype(jnp.float32)).astype(x.dtype)


def layer_norm(x, g, b):
    xf = x.astype(jnp.float32)
    mu = jnp.mean(xf, axis=-1, keepdims=True)
    var = jnp.mean(jnp.square(xf - mu), axis=-1, keepdims=True)
    y = (xf - mu) * lax.rsqrt(var + EPS)
    return (y * g.astype(jnp.float32) + b.astype(jnp.float32)).astype(x.dtype)


def modulate(xn, shift, scale):
    return xn * (1 + scale) + shift


def adaln(cond, w_ada, b_ada, n):
    m = jax.nn.silu(cond) @ w_ada[:, :n * D_MODEL] + b_ada[:n * D_MODEL]
    return jnp.split(m, n, axis=-1)


def split_heads(t, n):
    return t.reshape(t.shape[0], t.shape[1], n, HEAD_DIM)


def rope_axis(x, pos):
    half = x.shape[-1] // 2
    freqs = ROPE_THETA ** (-jnp.arange(half, dtype=jnp.float32) / half)
    ang = pos.astype(jnp.float32)[:, None] * freqs[None, :]
    cos = jnp.cos(ang)[:, None, :]
    sin = jnp.sin(ang)[:, None, :]
    xf = x.astype(jnp.float32)
    x1, x2 = xf[..., :half], xf[..., half:]
    return jnp.concatenate([x1 * cos - x2 * sin, x2 * cos + x1 * sin], axis=-1).astype(x.dtype)


def axial_rope(x, row, col):
    h = x.shape[-1] // 2
    return jnp.concatenate([rope_axis(x[..., :h], row), rope_axis(x[..., h:], col)], axis=-1)


def chunk_mlp(z, ln_g, ln_b, w_s, b_s):
    bsz, L, _ = z.shape
    z = jax.nn.gelu(z, approximate=False)
    u, v = jnp.split(z, 2, axis=-1)
    v = layer_norm(v, ln_g, ln_b).reshape(bsz, L // CHUNK, CHUNK, A_GROUPS, A_GROUP_DIM)
    mixed = jnp.einsum('gpq,bnqgc->bnpgc', w_s, v) + b_s.T[:, :, None]
    return u * mixed.reshape(bsz, L, A_WIDTH)


def attend(qi, k, v):
    s = jnp.einsum('bqkgd,bskd->bkgqs', qi, k).astype(jnp.float32) * (HEAD_DIM ** -0.5)
    p = jax.nn.softmax(s, axis=-1).astype(v.dtype)
    return jnp.einsum('bkgqs,bskd->bqkgd', p, v)


def latent_attention(q, k_all, v_all):
    bsz, S = q.shape[:2]
    qb = q.reshape(bsz, S // Q_BLOCK, Q_BLOCK, N_KV_HEADS, KV_GROUP, HEAD_DIM).transpose(1, 0, 2, 3, 4, 5)
    o = lax.map(lambda qi: attend(qi, k_all, v_all), qb)
    return o.transpose(1, 0, 2, 3, 4, 5).reshape(bsz, S, B_WIDTH)


def merge_mixers(out_a, out_b, g_out_a, g_out_b, w_out):
    return jnp.concatenate([rms_norm(out_a, g_out_a), rms_norm(out_b, g_out_b)], axis=-1) @ w_out


def depthwise_conv(h, w, b):
    L = h.shape[1]
    pad = CONV_W // 2
    hp = jnp.pad(h, ((0, 0), (pad, pad), (0, 0)))
    return sum(hp[:, j:j + L] * w[j] for j in range(CONV_W)) + b


def ffn_sublayer(x, g_pre, g_post, shift, scale, gate, w_up, conv_w, conv_b, w_down):
    h = modulate(rms_norm(x, g_pre), shift, scale)
    a = depthwise_conv(h @ w_up, conv_w, conv_b)
    gt, up = jnp.split(a, 2, axis=-1)
    f = (jax.nn.silu(gt) * up) @ w_down
    return x + gate * rms_norm(f, g_post)


def setup_inputs(seed: int = 0) -> dict:
    key = jax.random.key(seed)
    ks = jax.random.split(key, 28)
    f32 = jnp.float32

    def nrm(k, shape, scale):
        return jax.random.normal(k, shape, f32) * scale

    def gain(k, shape):
        return 1.0 + 0.05 * jax.random.normal(k, shape, f32)

    L = DEPTH
    return {
        'x': nrm(ks[0], (BATCH, SEQ, D_MODEL), 1.0),
        'c': nrm(ks[1], (BATCH, D_MODEL), 1.0),
        'ctx': nrm(ks[2], (BATCH, CTX_LEN, D_MODEL), 1.0),
        'c_ctx': nrm(ks[3], (D_MODEL,), 1.0),
        'w_ada': nrm(ks[4], (L, D_MODEL, N_MOD * D_MODEL), 0.5 * D_MODEL ** -0.5),
        'b_ada': nrm(ks[5], (L, N_MOD * D_MODEL), 0.02),
        'g_pre_mix': gain(ks[6], (L, D_MODEL)),
        'g_post_mix': gain(ks[7], (L, D_MODEL)),
        'g_pre_ffn': gain(ks[8], (L, D_MODEL)),
        'g_post_ffn': gain(ks[9], (L, D_MODEL)),
        'w_in': nrm(ks[10], (L, D_MODEL, IN_COLS), D_MODEL ** -0.5),
        'ln_v_g': gain(ks[11], (L, A_WIDTH)),
        'ln_v_b': nrm(ks[12], (L, A_WIDTH), 0.02),
        'w_s': nrm(ks[13], (L, A_GROUPS, CHUNK, CHUNK), CHUNK ** -0.5),
        'b_s': gain(ks[14], (L, A_GROUPS, CHUNK)),
        'g_q': gain(ks[15], (L, HEAD_DIM)),
        'g_k': gain(ks[16], (L, HEAD_DIM)),
        'g_out_a': gain(ks[17], (L, A_WIDTH)),
        'g_out_b': gain(ks[18], (L, B_WIDTH)),
        'w_out': nrm(ks[19], (L, MIX_WIDTH, D_MODEL), MIX_WIDTH ** -0.5),
        'w_up': nrm(ks[20], (L, D_MODEL, 2 * D_FF), D_MODEL ** -0.5),
        'conv_w': nrm(ks[21], (L, CONV_W, 2 * D_FF), CONV_W ** -0.5),
        'conv_b': nrm(ks[22], (L, 2 * D_FF), 0.02),
        'w_down': nrm(ks[23], (L, D_FF, D_MODEL), D_FF ** -0.5),
    }


def reference(x, c, ctx, c_ctx, w_ada, b_ada, g_pre_mix, g_post_mix, g_pre_ffn, g_post_ffn,
              w_in, ln_v_g, ln_v_b, w_s, b_s, g_q, g_k, g_out_a, g_out_b, w_out,
              w_up, conv_w, conv_b, w_down):
    bsz, S, _ = x.shape
    C = ctx.shape[1]
    ROWS = S // GRID_W
    row = jnp.broadcast_to(jnp.arange(ROWS, dtype=jnp.int32)[:, None], (ROWS, GRID_W)).reshape(-1)
    col = jnp.broadcast_to(jnp.arange(GRID_W, dtype=jnp.int32)[None, :], (ROWS, GRID_W)).reshape(-1)

    for l in range(DEPTH):
        last = l == DEPTH - 1
        sh_a, sc_a, gt_a, sh_f, sc_f, gt_f = [m[:, None, :] for m in adaln(c, w_ada[l], b_ada[l], N_MOD)]
        cmods = adaln(c_ctx, w_ada[l], b_ada[l], 2 if last else N_MOD)

        h = modulate(rms_norm(x, g_pre_mix[l]), sh_a, sc_a)
        hc = modulate(rms_norm(ctx, g_pre_mix[l]), cmods[0], cmods[1])

        za, q, k, v = jnp.split(h @ w_in[l], SPLITS, axis=-1)
        q = axial_rope(rms_norm(split_heads(q, N_HEADS), g_q[l]), row, col)
        k = axial_rope(rms_norm(split_heads(k, N_KV_HEADS), g_k[l]), row, col)
        v = split_heads(v, N_KV_HEADS)
        if last:
            kc, vc = jnp.split(hc @ w_in[l][:, SPLITS[1]:], 2, axis=-1)
        else:
            zac, qc, kc, vc = jnp.split(hc @ w_in[l], SPLITS, axis=-1)
        kc = rms_norm(split_heads(kc, N_KV_HEADS), g_k[l])
        vc = split_heads(vc, N_KV_HEADS)
        k_all = jnp.concatenate([kc, k], axis=1)
        v_all = jnp.concatenate([vc, v], axis=1)

        out_a = chunk_mlp(za, ln_v_g[l], ln_v_b[l], w_s[l], b_s[l])
        out_b = latent_attention(q, k_all, v_all)
        o = merge_mixers(out_a, out_b, g_out_a[l], g_out_b[l], w_out[l])
        x_mix = x + gt_a * rms_norm(o, g_post_mix[l])

        if not last:
            out_ac = chunk_mlp(zac, ln_v_g[l], ln_v_b[l], w_s[l], b_s[l])
            qc = rms_norm(split_heads(qc, N_HEADS), g_q[l])
            out_bc = attend(qc.reshape(bsz, C, N_KV_HEADS, KV_GROUP, HEAD_DIM), kc, vc).reshape(bsz, C, B_WIDTH)
            oc = merge_mixers(out_ac, out_bc, g_out_a[l], g_out_b[l], w_out[l])
            ctx_mix = ctx + cmods[2] * rms_norm(oc, g_post_mix[l])
            ctx = ffn_sublayer(ctx_mix, g_pre_ffn[l], g_post_ffn[l], cmods[3], cmods[4], cmods[5],
                               w_up[l], conv_w[l], conv_b[l], w_down[l])

        x = ffn_sublayer(x_mix, g_pre_ffn[l], g_post_ffn[l], sh_f, sc_f, gt_f,
                         w_up[l], conv_w[l], conv_b[l], w_down[l])
    return x
```

```python
import functools

import numpy as np
import jax
import jax.numpy as jnp
from jax import lax
from jax.experimental import pallas as pl
from jax.experimental.pallas import tpu as pltpu

GRID_W = 64
HEAD_DIM = 128
ROPE_THETA = 10000.0
EPS = 1e-6
N_MOD = 6

LANES = 128
SUBLANES = 8
BF16_ROWS = 16
VMEM_LIMIT_BYTES = 56 * 1024 * 1024

ADALN_TN = 512
ROW_TILE = 256
INPROJ_TM = 512
INPROJ_TN = 1024
GMLP_CHUNKS_PER_STEP = 2
ATTN_TQ = 256
ATTN_TK = 1024
OUTPROJ_TM = 512
OUTPROJ_TN = 1024
FFN_TM = 512
FFN_TF = 256

F32 = jnp.float32
BF16 = jnp.bfloat16


def _params(n_axes):
    return pltpu.CompilerParams(dimension_semantics=("arbitrary",) * n_axes,
                                vmem_limit_bytes=VMEM_LIMIT_BYTES)


def _rms(x, g):
    return x * lax.rsqrt(jnp.mean(x * x, axis=-1, keepdims=True) + EPS) * g


def _adaln_kernel(c_ref, w_ref, b_ref, o_ref, s_ref):
    k_dim, tn = w_ref.shape
    nj = tn // LANES

    @pl.when(pl.program_id(0) == 0)
    def _():
        c = c_ref[...]
        s_ref[...] = c * jax.nn.sigmoid(c)

    def body(r, accs):
        rows = pl.ds(pl.multiple_of(r * SUBLANES, SUBLANES), SUBLANES)
        s0 = s_ref[0, rows, :]
        s1 = s_ref[1, rows, :]
        new = []
        for j in range(nj):
            w = w_ref[rows, j * LANES:(j + 1) * LANES]
            new.append(accs[2 * j] + w * s0)
            new.append(accs[2 * j + 1] + w * s1)
        return tuple(new)

    zero = jnp.zeros((SUBLANES, LANES), F32)
    accs = lax.fori_loop(0, k_dim // SUBLANES, body, (zero,) * (2 * nj), unroll=4)
    for j in range(nj):
        cols = slice(j * LANES, (j + 1) * LANES)
        o_ref[0:1, cols] = jnp.sum(accs[2 * j], axis=0, keepdims=True) + b_ref[:, cols]
        o_ref[1:2, cols] = jnp.sum(accs[2 * j + 1], axis=0, keepdims=True) + b_ref[:, cols]


def _adaln(cc, w_ada, b_ada):
    k_dim, n = w_ada.shape
    tn = min(ADALN_TN, n)
    c_rep = jnp.broadcast_to(cc[:, :, None], (2, k_dim, LANES))
    return pl.pallas_call(
        _adaln_kernel,
        grid=(n // tn,),
        in_specs=[pl.BlockSpec((2, k_dim, LANES), lambda j: (0, 0, 0)),
                  pl.BlockSpec((k_dim, tn), lambda j: (0, j)),
                  pl.BlockSpec((1, tn), lambda j: (0, j))],
        out_specs=pl.BlockSpec((2, tn), lambda j: (0, j)),
        out_shape=jax.ShapeDtypeStruct((2, n), F32),
        scratch_shapes=[pltpu.VMEM((2, k_dim, LANES), F32)],
        compiler_params=_params(1),
        name="adaln",
    )(c_rep, w_ada, b_ada.reshape(1, n))


def _prenorm_kernel(x_ref, g_ref, sh_ref, sc_ref, o_ref, *, row):
    y = _rms(x_ref[...], g_ref[...])
    o_ref[...] = (y * (1.0 + sc_ref[row:row + 1, :]) + sh_ref[row:row + 1, :]).astype(o_ref.dtype)


def _prenorm(x, g, mods, row, shift_idx, scale_idx):
    s, d = x.shape
    tm = min(ROW_TILE, s)
    return pl.pallas_call(
        functools.partial(_prenorm_kernel, row=row),
        grid=(s // tm,),
        in_specs=[pl.BlockSpec((tm, d), lambda i: (i, 0)),
                  pl.BlockSpec((1, d), lambda i: (0, 0)),
                  pl.BlockSpec((2, d), lambda i: (0, shift_idx)),
                  pl.BlockSpec((2, d), lambda i: (0, scale_idx))],
        out_specs=pl.BlockSpec((tm, d), lambda i: (i, 0)),
        out_shape=jax.ShapeDtypeStruct((s, d), BF16),
        compiler_params=_params(1),
        name="prenorm",
    )(x, g.reshape(1, d), mods, mods)


def _swap_rope_halves(a):
    lane = lax.broadcasted_iota(jnp.int32, a.shape, 1)
    return jnp.where((lane & 32) != 0, pltpu.roll(a, 32, 1), pltpu.roll(a, LANES - 32, 1))


def _head_epilogue(a, kind, cos, sin, gq, gk):
    if kind == "gelu":
        return 0.5 * a * (1.0 + lax.erf(a * (2.0 ** -0.5)))
    if kind == "v":
        return a
    a = _rms(a, gq if kind == "q" else gk)
    if kind != "kc":
        a = a * cos + _swap_rope_halves(a) * sin
    if kind == "q":
        a = a * (HEAD_DIM ** -0.5)
    return a


def _inproj_kernel(h_ref, w_ref, cos_ref, sin_ref, gq_ref, gk_ref, o_ref, *, plans):
    j = pl.program_id(0)
    acc = jnp.dot(h_ref[...], w_ref[...], preferred_element_type=F32)
    for kinds, tiles in plans:
        cond = functools.reduce(jnp.logical_or, [j == t for t in tiles])

        @pl.when(cond)
        def _(kinds=kinds):
            for c, kind in enumerate(kinds):
                cols = slice(c * LANES, (c + 1) * LANES)
                o_ref[:, cols] = _head_epilogue(acc[:, cols], kind, cos_ref[...], sin_ref[...],
                                                gq_ref[...], gk_ref[...]).astype(o_ref.dtype)


def _inproj(h, w, col0, kinds_per_chunk, cos_t, sin_t, gq, gk):
    s, d = h.shape
    n = LANES * len(kinds_per_chunk)
    tm = min(INPROJ_TM, s)
    tn = min(INPROJ_TN, n)
    assert n % tn == 0 and col0 % tn == 0 and s % tm == 0
    per_tile = tn // LANES
    by_kinds = {}
    for t in range(n // tn):
        by_kinds.setdefault(tuple(kinds_per_chunk[t * per_tile:(t + 1) * per_tile]), []).append(t)
    plans = tuple((k, tuple(v)) for k, v in by_kinds.items())
    j0 = col0 // tn
    return pl.pallas_call(
        functools.partial(_inproj_kernel, plans=plans),
        grid=(n // tn, s // tm),
        in_specs=[pl.BlockSpec((tm, d), lambda j, i: (i, 0)),
                  pl.BlockSpec((d, tn), lambda j, i: (0, j + j0)),
                  pl.BlockSpec((tm, LANES), lambda j, i: (i, 0)),
                  pl.BlockSpec((tm, LANES), lambda j, i: (i, 0)),
                  pl.BlockSpec((1, LANES), lambda j, i: (0, 0)),
                  pl.BlockSpec((1, LANES), lambda j, i: (0, 0))],
        out_specs=pl.BlockSpec((tm, tn), lambda j, i: (i, j)),
        out_shape=jax.ShapeDtypeStruct((s, n), BF16),
        compiler_params=_params(2),
        name="inproj",
    )(h, w, cos_t, sin_t, gq.reshape(1, LANES), gk.reshape(1, LANES))


def _gmlp_kernel(z_ref, lng_ref, lnb_ref, ws_ref, bs_ref, go_ref, o_ref, oa_ref, *, chunk, groups):
    aw = groups * LANES
    for cidx in range(z_ref.shape[0] // chunk):
        rows = slice(cidx * chunk, (cidx + 1) * chunk)
        v = z_ref[rows, aw:].astype(F32)
        mu = jnp.mean(v, axis=-1, keepdims=True)
        vc = v - mu
        var = jnp.mean(vc * vc, axis=-1, keepdims=True)
        vn = (vc * lax.rsqrt(var + EPS) * lng_ref[...] + lnb_ref[...]).astype(BF16)
        for g in range(groups):
            cols = slice(g * LANES, (g + 1) * LANES)
            mixed = jnp.dot(ws_ref[g], vn[:, cols], preferred_element_type=F32) + bs_ref[g]
            oa_ref[:, cols] = z_ref[rows, cols].astype(F32) * mixed
        o_ref[rows, :] = _rms(oa_ref[...], go_ref[...]).astype(o_ref.dtype)


def _gmlp(z, n_cols_a, ln_g, ln_b, w_s, b_s, g_out):
    s = z.shape[0]
    groups, chunk, _ = w_s.shape
    aw = n_cols_a // 2
    assert aw == groups * LANES
    rows = chunk * min(GMLP_CHUNKS_PER_STEP, s // chunk)
    bs_rep = jnp.broadcast_to(b_s[:, :, None], (groups, chunk, LANES))
    return pl.pallas_call(
        functools.partial(_gmlp_kernel, chunk=chunk, groups=groups),
        grid=(s // rows,),
        in_specs=[pl.BlockSpec((rows, 2 * aw), lambda i: (i, 0)),
                  pl.BlockSpec((1, aw), lambda i: (0, 0)),
                  pl.BlockSpec((1, aw), lambda i: (0, 0)),
                  pl.BlockSpec((groups, chunk, chunk), lambda i: (0, 0, 0)),
                  pl.BlockSpec((groups, chunk, LANES), lambda i: (0, 0, 0)),
                  pl.BlockSpec((1, aw), lambda i: (0, 0))],
        out_specs=pl.BlockSpec((rows, aw), lambda i: (i, 0)),
        out_shape=jax.ShapeDtypeStruct((s, aw), BF16),
        scratch_shapes=[pltpu.VMEM((chunk, aw), F32)],
        compiler_params=_params(1),
        name="gmlp",
    )(z, ln_g.reshape(1, aw), ln_b.reshape(1, aw), w_s.astype(BF16), bs_rep, g_out.reshape(1, aw))


def _attn_kernel(q_ref, kc_ref, vc_ref, k_ref, v_ref, go_ref, o_ref, m_ref, l_ref, acc_ref, ob_ref,
                 *, group, tk):
    kh = pl.program_id(1)
    n_kh = pl.num_programs(1)
    tq = q_ref.shape[0]
    q = jnp.concatenate([q_ref[:, g * HEAD_DIM:(g + 1) * HEAD_DIM] for g in range(group)], axis=0)
    m_ref[...] = jnp.full(m_ref.shape, -jnp.inf, F32)
    l_ref[...] = jnp.zeros(l_ref.shape, F32)
    acc_ref[...] = jnp.zeros(acc_ref.shape, F32)

    def step(k, v):
        s = lax.dot_general(q, k, (((1,), (1,)), ((), ())), preferred_element_type=F32)
        m_old = m_ref[...]
        m_new = jnp.maximum(m_old, jnp.max(s, axis=-1, keepdims=True))
        p = jnp.exp(s - m_new)
        alpha = jnp.exp(m_old - m_new)
        l_ref[...] = alpha * l_ref[...] + jnp.sum(p, axis=-1, keepdims=True)
        acc_ref[...] = alpha * acc_ref[...] + jnp.dot(p.astype(BF16), v, preferred_element_type=F32)
        m_ref[...] = m_new

    step(kc_ref[...], vc_ref[...])
    for c in range(k_ref.shape[0] // tk):
        step(k_ref[c * tk:(c + 1) * tk, :], v_ref[c * tk:(c + 1) * tk, :])

    out = acc_ref[...] / l_ref[...]
    for g in range(group):
        ob_ref[kh * group + g] = out[g * tq:(g + 1) * tq, :]

    @pl.when(kh == n_kh - 1)
    def _():
        n_heads = ob_ref.shape[0]
        ssq = jnp.zeros((tq, 1), F32)
        for h in range(n_heads):
            o = ob_ref[h]
            ssq = ssq + jnp.sum(o * o, axis=-1, keepdims=True)
        inv = lax.rsqrt(ssq / (n_heads * HEAD_DIM) + EPS)
        for h in range(n_heads):
            cols = slice(h * HEAD_DIM, (h + 1) * HEAD_DIM)
            o_ref[:, cols] = (ob_ref[h] * inv * go_ref[:, cols]).astype(o_ref.dtype)


def _attention(z, q_col0, k_col0, v_col0, n_heads, n_kv, zc, g_out):
    s = z.shape[0]
    c = zc.shape[0]
    group = n_heads // n_kv
    gw = group * HEAD_DIM
    bw = n_heads * HEAD_DIM
    tq = min(ATTN_TQ, s)
    tk = min(ATTN_TK, s)
    assert q_col0 % gw == 0 and k_col0 % HEAD_DIM == 0 and v_col0 % HEAD_DIM == 0
    qb, kb, vb = q_col0 // gw, k_col0 // HEAD_DIM, v_col0 // HEAD_DIM
    return pl.pallas_call(
        functools.partial(_attn_kernel, group=group, tk=tk),
        grid=(s // tq, n_kv),
        in_specs=[pl.BlockSpec((tq, gw), lambda i, kh: (i, qb + kh)),
                  pl.BlockSpec((c, HEAD_DIM), lambda i, kh: (0, kh)),
                  pl.BlockSpec((c, HEAD_DIM), lambda i, kh: (0, n_kv + kh)),
                  pl.BlockSpec((s, HEAD_DIM), lambda i, kh: (0, kb + kh)),
                  pl.BlockSpec((s, HEAD_DIM), lambda i, kh: (0, vb + kh)),
                  pl.BlockSpec((1, bw), lambda i, kh: (0, 0))],
        out_specs=pl.BlockSpec((tq, bw), lambda i, kh: (i, 0)),
        out_shape=jax.ShapeDtypeStruct((s, bw), BF16),
        scratch_shapes=[pltpu.VMEM((group * tq, 1), F32),
                        pltpu.VMEM((group * tq, 1), F32),
                        pltpu.VMEM((group * tq, HEAD_DIM), F32),
                        pltpu.VMEM((n_heads, tq, HEAD_DIM), F32)],
        compiler_params=_params(2),
        name="attn",
    )(z, zc, zc, z, z, g_out.reshape(1, bw))


def _outproj_kernel(a_ref, b_ref, wa_ref, wb_ref, o_ref):
    o_ref[...] = (jnp.dot(a_ref[...], wa_ref[...], preferred_element_type=F32)
                  + jnp.dot(b_ref[...], wb_ref[...], preferred_element_type=F32))


def _outproj(a, b, w):
    s, aw = a.shape
    bw = b.shape[1]
    n = w.shape[1]
    tm = min(OUTPROJ_TM, s)
    tn = min(OUTPROJ_TN, n)
    assert aw == bw
    return pl.pallas_call(
        _outproj_kernel,
        grid=(n // tn, s // tm),
        in_specs=[pl.BlockSpec((tm, aw), lambda j, i: (i, 0)),
                  pl.BlockSpec((tm, bw), lambda j, i: (i, 0)),
                  pl.BlockSpec((aw, tn), lambda j, i: (0, j)),
                  pl.BlockSpec((bw, tn), lambda j, i: (1, j))],
        out_specs=pl.BlockSpec((tm, tn), lambda j, i: (i, j)),
        out_shape=jax.ShapeDtypeStruct((s, n), F32),
        compiler_params=_params(2),
        name="outproj",
    )(a, b, w, w)


def _post_kernel(x_ref, o_ref, gpost_ref, gate_ref, gpre_ref, sh_ref, sc_ref, xm_ref, h_ref):
    xm = x_ref[...] + gate_ref[0:1, :] * _rms(o_ref[...], gpost_ref[...])
    xm_ref[...] = xm
    h = _rms(xm, gpre_ref[...])
    h_ref[...] = (h * (1.0 + sc_ref[0:1, :]) + sh_ref[0:1, :]).astype(h_ref.dtype)


def _post(x, o, g_post, mods, g_pre):
    s, d = x.shape
    tm = min(ROW_TILE, s)
    row = pl.BlockSpec((tm, d), lambda i: (i, 0))
    vec = pl.BlockSpec((1, d), lambda i: (0, 0))

    def mod(idx):
        return pl.BlockSpec((2, d), lambda i: (0, idx))

    return pl.pallas_call(
        _post_kernel,
        grid=(s // tm,),
        in_specs=[row, row, vec, mod(2), vec, mod(3), mod(4)],
        out_specs=[row, row],
        out_shape=[jax.ShapeDtypeStruct((s, d), F32), jax.ShapeDtypeStruct((s, d), BF16)],
        compiler_params=_params(1),
        name="post",
    )(x, o, g_post.reshape(1, d), mods, g_pre.reshape(1, d), mods, mods)


def _ffn_kernel(hp_ref, hm_ref, hn_ref, wg_ref, wu_ref, cwg_ref, cwu_ref, cbg_ref, cbu_ref, wd_ref,
                o_ref, lhs_ref, acc_ref):
    i = pl.program_id(0)
    f = pl.program_id(1)
    tm = hm_ref.shape[0]
    halo = hp_ref.shape[0]

    @pl.when(f == 0)
    def _():
        lhs_ref[0:halo, :] = jnp.where(i > 0, hp_ref[...], jnp.zeros_like(hp_ref[...]))
        lhs_ref[halo:halo + tm, :] = hm_ref[...]
        lhs_ref[halo + tm:, :] = jnp.where(i < pl.num_programs(0) - 1, hn_ref[...],
                                           jnp.zeros_like(hn_ref[...]))
        acc_ref[...] = jnp.zeros(acc_ref.shape, F32)

    lhs = lhs_ref[...]
    rows = lhs.shape[0]

    def conv(w_ref, cw_ref, cb_ref):
        r = jnp.dot(lhs, w_ref[...], preferred_element_type=F32)
        prev = pltpu.roll(r, 1, 0)[halo:halo + tm, :]
        nxt = pltpu.roll(r, rows - 1, 0)[halo:halo + tm, :]
        return (prev * cw_ref[0:1, :] + r[halo:halo + tm, :] * cw_ref[1:2, :]
                + nxt * cw_ref[2:3, :] + cb_ref[...])

    gate = conv(wg_ref, cwg_ref, cbg_ref)
    up = conv(wu_ref, cwu_ref, cbu_ref)
    act = (gate * jax.nn.sigmoid(gate) * up).astype(BF16)
    acc_ref[...] += jnp.dot(act, wd_ref[...], preferred_element_type=F32)

    @pl.when(f == pl.num_programs(1) - 1)
    def _():
        o_ref[...] = acc_ref[...].astype(o_ref.dtype)


def _ffn(h, w_gu, conv_w, conv_b, w_down):
    s, d = h.shape
    ff = w_down.shape[0]
    tm = min(FFN_TM, s)
    tf = min(FFN_TF, ff)
    assert ff % tf == 0 and s % tm == 0 and tm % BF16_ROWS == 0
    nf = ff // tf
    hb = tm // BF16_ROWS
    n_hb = s // BF16_ROWS
    return pl.pallas_call(
        _ffn_kernel,
        grid=(s // tm, nf),
        in_specs=[pl.BlockSpec((BF16_ROWS, d), lambda i, f: (jnp.maximum(i * hb - 1, 0), 0)),
                  pl.BlockSpec((tm, d), lambda i, f: (i, 0)),
                  pl.BlockSpec((BF16_ROWS, d), lambda i, f: (jnp.minimum((i + 1) * hb, n_hb - 1), 0)),
                  pl.BlockSpec((d, tf), lambda i, f: (0, f)),
                  pl.BlockSpec((d, tf), lambda i, f: (0, nf + f)),
                  pl.BlockSpec((3, tf), lambda i, f: (0, f)),
                  pl.BlockSpec((3, tf), lambda i, f: (0, nf + f)),
                  pl.BlockSpec((1, tf), lambda i, f: (0, f)),
                  pl.BlockSpec((1, tf), lambda i, f: (0, nf + f)),
                  pl.BlockSpec((tf, d), lambda i, f: (f, 0))],
        out_specs=pl.BlockSpec((tm, d), lambda i, f: (i, 0)),
        out_shape=jax.ShapeDtypeStruct((s, d), BF16),
        scratch_shapes=[pltpu.VMEM((tm + 2 * BF16_ROWS, d), BF16),
                        pltpu.VMEM((tm, d), F32)],
        compiler_params=_params(2),
        name="ffn",
    )(h, h, h, w_gu, w_gu, conv_w, conv_w, conv_b, conv_b, w_down)


def _final_kernel(xm_ref, f_ref, g_ref, gate_ref, o_ref):
    o_ref[...] = xm_ref[...] + gate_ref[0:1, :] * _rms(f_ref[...].astype(F32), g_ref[...])


def _final(xm, f, g_post, mods):
    s, d = xm.shape
    tm = min(ROW_TILE, s)
    row = pl.BlockSpec((tm, d), lambda i: (i, 0))
    return pl.pallas_call(
        _final_kernel,
        grid=(s // tm,),
        in_specs=[row, row, pl.BlockSpec((1, d), lambda i: (0, 0)),
                  pl.BlockSpec((2, d), lambda i: (0, 5))],
        out_specs=row,
        out_shape=jax.ShapeDtypeStruct((s, d), F32),
        compiler_params=_params(1),
        name="final",
    )(xm, f, g_post.reshape(1, d), mods)


def _rope_tables(s):
    quarter = HEAD_DIM // 4
    pos = np.arange(s)
    freqs = ROPE_THETA ** (-np.arange(quarter, dtype=np.float64) / quarter)
    ang_r = (pos // GRID_W)[:, None] * freqs[None, :]
    ang_c = (pos % GRID_W)[:, None] * freqs[None, :]
    cos_t = np.concatenate([np.cos(ang_r), np.cos(ang_r), np.cos(ang_c), np.cos(ang_c)], axis=1)
    sin_t = np.concatenate([-np.sin(ang_r), np.sin(ang_r), -np.sin(ang_c), np.sin(ang_c)], axis=1)
    return jnp.asarray(cos_t, F32), jnp.asarray(sin_t, F32)


def kernel(x, c, ctx, c_ctx, w_ada, b_ada, g_pre_mix, g_post_mix, g_pre_ffn, g_post_ffn, w_in, ln_v_g,
           ln_v_b, w_s, b_s, g_q, g_k, g_out_a, g_out_b, w_out, w_up, conv_w, conv_b, w_down):
    bsz, s, d = x.shape
    assert bsz == 1 and w_in.shape[0] == 1, "single batch element, single layer"
    aw = ln_v_g.shape[-1]
    bw = g_out_b.shape[-1]
    in_cols = w_in.shape[-1]
    kvw = (in_cols - 2 * aw - bw) // 2
    n_heads, n_kv = bw // HEAD_DIM, kvw // HEAD_DIM
    q0, k0, v0 = 2 * aw, 2 * aw + bw, 2 * aw + bw + kvw
    ff = w_down.shape[1]

    x2 = x[0]
    ctx2 = ctx[0]
    w_in_b = w_in[0].astype(BF16)
    w_out_b = w_out[0].astype(BF16)
    w_gu_b = w_up[0].astype(BF16)
    w_down_b = w_down[0].astype(BF16)

    mods = _adaln(jnp.concatenate([c, c_ctx[None, :]], axis=0), w_ada[0], b_ada[0])

    cos_t, sin_t = _rope_tables(s)
    h = _prenorm(x2, g_pre_mix[0], mods, 0, 0, 1)
    hc = _prenorm(ctx2, g_pre_mix[0], mods, 1, 0, 1)

    kinds = (["gelu"] * (2 * aw // LANES) + ["q"] * n_heads + ["k"] * n_kv + ["v"] * n_kv)
    z = _inproj(h, w_in_b, 0, kinds, cos_t, sin_t, g_q[0], g_k[0])
    zc = _inproj(hc, w_in_b, k0, ["kc"] * n_kv + ["v"] * n_kv, cos_t, sin_t, g_q[0], g_k[0])

    out_a = _gmlp(z, 2 * aw, ln_v_g[0], ln_v_b[0], w_s[0], b_s[0], g_out_a[0])
    out_b = _attention(z, q0, k0, v0, n_heads, n_kv, zc, g_out_b[0])
    o = _outproj(out_a, out_b, w_out_b)
    x_mix, h2 = _post(x2, o, g_post_mix[0], mods, g_pre_ffn[0])
    f = _ffn(h2, w_gu_b, conv_w[0], conv_b[0].reshape(1, 2 * ff), w_down_b)
    out = _final(x_mix, f, g_post_ffn[0], mods)
    return out[None]
```

```python
import functools

import numpy as np
import jax
import jax.numpy as jnp
from jax import lax
from jax.experimental import pallas as pl
from jax.experimental.pallas import tpu as pltpu

GRID_W = 64
HEAD_DIM = 128
ROPE_THETA = 10000.0
EPS = 1e-6
N_MOD = 6
LOG2_E = 1.4426950408889634

LANES = 128
SUBLANES = 8
BF16_ROWS = 16
VMEM_LIMIT_BYTES = 56 * 1024 * 1024

ADALN_TN = 512
ROW_TILE = 256
INPROJ_TM = 512
INPROJ_TN = 1024
GMLP_CHUNKS_PER_STEP = 2
ATTN_TQ = 256
ATTN_TK = 1024
OUTPROJ_TM = 512
OUTPROJ_TN = 1024
FFN_TM = 1024
FFN_TF = 256
FFN_ACT_ROWS = 256

F32 = jnp.float32
BF16 = jnp.bfloat16


def _params(n_axes):
    return pltpu.CompilerParams(dimension_semantics=("arbitrary",) * n_axes,
                                vmem_limit_bytes=VMEM_LIMIT_BYTES)


def _rms(x, g):
    return x * lax.rsqrt(jnp.mean(x * x, axis=-1, keepdims=True) + EPS) * g


def _adaln_kernel(c_ref, w_ref, b_ref, o_ref, s_ref):
    k_dim, tn = w_ref.shape
    nj = tn // LANES

    @pl.when(pl.program_id(0) == 0)
    def _():
        c = c_ref[...]
        s_ref[...] = c * jax.nn.sigmoid(c)

    def body(r, accs):
        rows = pl.ds(pl.multiple_of(r * SUBLANES, SUBLANES), SUBLANES)
        s0 = s_ref[0, rows, :]
        s1 = s_ref[1, rows, :]
        new = []
        for j in range(nj):
            w = w_ref[rows, j * LANES:(j + 1) * LANES]
            new.append(accs[2 * j] + w * s0)
            new.append(accs[2 * j + 1] + w * s1)
        return tuple(new)

    zero = jnp.zeros((SUBLANES, LANES), F32)
    accs = lax.fori_loop(0, k_dim // SUBLANES, body, (zero,) * (2 * nj), unroll=4)
    for j in range(nj):
        cols = slice(j * LANES, (j + 1) * LANES)
        o_ref[0:1, cols] = jnp.sum(accs[2 * j], axis=0, keepdims=True) + b_ref[:, cols]
        o_ref[1:2, cols] = jnp.sum(accs[2 * j + 1], axis=0, keepdims=True) + b_ref[:, cols]


def _adaln(cc, w_ada, b_ada):
    k_dim, n = w_ada.shape
    tn = min(ADALN_TN, n)
    c_rep = jnp.broadcast_to(cc[:, :, None], (2, k_dim, LANES))
    return pl.pallas_call(
        _adaln_kernel,
        grid=(n // tn,),
        in_specs=[pl.BlockSpec((2, k_dim, LANES), lambda j: (0, 0, 0)),
                  pl.BlockSpec((k_dim, tn), lambda j: (0, j)),
                  pl.BlockSpec((1, tn), lambda j: (0, j))],
        out_specs=pl.BlockSpec((2, tn), lambda j: (0, j)),
        out_shape=jax.ShapeDtypeStruct((2, n), F32),
        scratch_shapes=[pltpu.VMEM((2, k_dim, LANES), F32)],
        compiler_params=_params(1),
        name="adaln",
    )(c_rep, w_ada, b_ada.reshape(1, n))


def _prenorm_kernel(x_ref, g_ref, sh_ref, sc_ref, o_ref, *, row):
    y = _rms(x_ref[...], g_ref[...])
    o_ref[...] = (y * (1.0 + sc_ref[row:row + 1, :]) + sh_ref[row:row + 1, :]).astype(o_ref.dtype)


def _prenorm(x, g, mods, row, shift_idx, scale_idx):
    s, d = x.shape
    tm = min(ROW_TILE, s)
    return pl.pallas_call(
        functools.partial(_prenorm_kernel, row=row),
        grid=(s // tm,),
        in_specs=[pl.BlockSpec((tm, d), lambda i: (i, 0)),
                  pl.BlockSpec((1, d), lambda i: (0, 0)),
                  pl.BlockSpec((2, d), lambda i: (0, shift_idx)),
                  pl.BlockSpec((2, d), lambda i: (0, scale_idx))],
        out_specs=pl.BlockSpec((tm, d), lambda i: (i, 0)),
        out_shape=jax.ShapeDtypeStruct((s, d), BF16),
        compiler_params=_params(1),
        name="prenorm",
    )(x, g.reshape(1, d), mods, mods)


def _swap_rope_halves(a):
    lane = lax.broadcasted_iota(jnp.int32, a.shape, 1)
    return jnp.where((lane & 32) != 0, pltpu.roll(a, 32, 1), pltpu.roll(a, LANES - 32, 1))


def _head_epilogue(a, kind, cos, sin, gq, gk):
    if kind == "gelu":
        return 0.5 * a * (1.0 + lax.erf(a * (2.0 ** -0.5)))
    if kind == "v":
        return a
    a = _rms(a, gq if kind == "q" else gk)
    if kind != "kc":
        a = a * cos + _swap_rope_halves(a) * sin
    if kind == "q":
        a = a * (HEAD_DIM ** -0.5 * LOG2_E)
    return a


def _inproj_kernel(h_ref, w_ref, cos_ref, sin_ref, gq_ref, gk_ref, o_ref, *, plans):
    j = pl.program_id(0)
    acc = jnp.dot(h_ref[...], w_ref[...], preferred_element_type=F32)
    for kinds, tiles in plans:
        cond = functools.reduce(jnp.logical_or, [j == t for t in tiles])

        @pl.when(cond)
        def _(kinds=kinds):
            for c, kind in enumerate(kinds):
                cols = slice(c * LANES, (c + 1) * LANES)
                o_ref[:, cols] = _head_epilogue(acc[:, cols], kind, cos_ref[...], sin_ref[...],
                                                gq_ref[...], gk_ref[...]).astype(o_ref.dtype)


def _inproj(h, w, col0, kinds_per_chunk, cos_t, sin_t, gq, gk):
    s, d = h.shape
    n = LANES * len(kinds_per_chunk)
    tm = min(INPROJ_TM, s)
    tn = min(INPROJ_TN, n)
    assert n % tn == 0 and col0 % tn == 0 and s % tm == 0
    per_tile = tn // LANES
    by_kinds = {}
    for t in range(n // tn):
        by_kinds.setdefault(tuple(kinds_per_chunk[t * per_tile:(t + 1) * per_tile]), []).append(t)
    plans = tuple((k, tuple(v)) for k, v in by_kinds.items())
    j0 = col0 // tn
    return pl.pallas_call(
        functools.partial(_inproj_kernel, plans=plans),
        grid=(n // tn, s // tm),
        in_specs=[pl.BlockSpec((tm, d), lambda j, i: (i, 0)),
                  pl.BlockSpec((d, tn), lambda j, i: (0, j + j0)),
                  pl.BlockSpec((tm, LANES), lambda j, i: (i, 0)),
                  pl.BlockSpec((tm, LANES), lambda j, i: (i, 0)),
                  pl.BlockSpec((1, LANES), lambda j, i: (0, 0)),
                  pl.BlockSpec((1, LANES), lambda j, i: (0, 0))],
        out_specs=pl.BlockSpec((tm, tn), lambda j, i: (i, j)),
        out_shape=jax.ShapeDtypeStruct((s, n), BF16),
        compiler_params=_params(2),
        name="inproj",
    )(h, w, cos_t, sin_t, gq.reshape(1, LANES), gk.reshape(1, LANES))


def _gmlp_kernel(z_ref, lng_ref, lnb_ref, ws_ref, bs_ref, go_ref, o_ref, oa_ref, *, chunk, groups):
    aw = groups * LANES
    for cidx in range(z_ref.shape[0] // chunk):
        rows = slice(cidx * chunk, (cidx + 1) * chunk)
        v = z_ref[rows, aw:].astype(F32)
        mu = jnp.mean(v, axis=-1, keepdims=True)
        vc = v - mu
        var = jnp.mean(vc * vc, axis=-1, keepdims=True)
        vn = (vc * lax.rsqrt(var + EPS) * lng_ref[...] + lnb_ref[...]).astype(BF16)
        for g in range(groups):
            cols = slice(g * LANES, (g + 1) * LANES)
            mixed = jnp.dot(ws_ref[g], vn[:, cols], preferred_element_type=F32) + bs_ref[g]
            oa_ref[:, cols] = z_ref[rows, cols].astype(F32) * mixed
        o_ref[rows, :] = _rms(oa_ref[...], go_ref[...]).astype(o_ref.dtype)


def _gmlp(z, n_cols_a, ln_g, ln_b, w_s, b_s, g_out):
    s = z.shape[0]
    groups, chunk, _ = w_s.shape
    aw = n_cols_a // 2
    assert aw == groups * LANES
    rows = chunk * min(GMLP_CHUNKS_PER_STEP, s // chunk)
    bs_rep = jnp.broadcast_to(b_s[:, :, None], (groups, chunk, LANES))
    return pl.pallas_call(
        functools.partial(_gmlp_kernel, chunk=chunk, groups=groups),
        grid=(s // rows,),
        in_specs=[pl.BlockSpec((rows, 2 * aw), lambda i: (i, 0)),
                  pl.BlockSpec((1, aw), lambda i: (0, 0)),
                  pl.BlockSpec((1, aw), lambda i: (0, 0)),
                  pl.BlockSpec((groups, chunk, chunk), lambda i: (0, 0, 0)),
                  pl.BlockSpec((groups, chunk, LANES), lambda i: (0, 0, 0)),
                  pl.BlockSpec((1, aw), lambda i: (0, 0))],
        out_specs=pl.BlockSpec((rows, aw), lambda i: (i, 0)),
        out_shape=jax.ShapeDtypeStruct((s, aw), BF16),
        scratch_shapes=[pltpu.VMEM((chunk, aw), F32)],
        compiler_params=_params(1),
        name="gmlp",
    )(z, ln_g.reshape(1, aw), ln_b.reshape(1, aw), w_s.astype(BF16), bs_rep, g_out.reshape(1, aw))


def _attn_kernel(q_ref, kc_ref, vc_ref, k_ref, v_ref, go_ref, o_ref, m_ref, l_ref, acc_ref, ob_ref,
                 *, group, tk):
    kh = pl.program_id(1)
    n_kh = pl.num_programs(1)
    tq = q_ref.shape[0]
    q = jnp.concatenate([q_ref[:, g * HEAD_DIM:(g + 1) * HEAD_DIM] for g in range(group)], axis=0)
    m_ref[...] = jnp.full(m_ref.shape, -jnp.inf, F32)
    l_ref[...] = jnp.zeros(l_ref.shape, F32)
    acc_ref[...] = jnp.zeros(acc_ref.shape, F32)

    def step(k, v):
        s = lax.dot_general(q, k, (((1,), (1,)), ((), ())), preferred_element_type=F32)
        m_old = m_ref[...]
        m_new = jnp.maximum(m_old, jnp.max(s, axis=-1, keepdims=True))
        p = jnp.exp2(s - m_new)
        alpha = jnp.exp2(m_old - m_new)
        l_ref[...] = alpha * l_ref[...] + jnp.sum(p, axis=-1, keepdims=True)
        acc_ref[...] = alpha * acc_ref[...] + jnp.dot(p.astype(BF16), v, preferred_element_type=F32)
        m_ref[...] = m_new

    step(kc_ref[...], vc_ref[...])
    for c in range(k_ref.shape[0] // tk):
        step(k_ref[c * tk:(c + 1) * tk, :], v_ref[c * tk:(c + 1) * tk, :])

    out = acc_ref[...] / l_ref[...]
    for g in range(group):
        ob_ref[kh * group + g] = out[g * tq:(g + 1) * tq, :]

    @pl.when(kh == n_kh - 1)
    def _():
        n_heads = ob_ref.shape[0]
        ssq = jnp.zeros((tq, 1), F32)
        for h in range(n_heads):
            o = ob_ref[h]
            ssq = ssq + jnp.sum(o * o, axis=-1, keepdims=True)
        inv = lax.rsqrt(ssq / (n_heads * HEAD_DIM) + EPS)
        for h in range(n_heads):
            cols = slice(h * HEAD_DIM, (h + 1) * HEAD_DIM)
            o_ref[:, cols] = (ob_ref[h] * inv * go_ref[:, cols]).astype(o_ref.dtype)


def _attention(z, q_col0, k_col0, v_col0, n_heads, n_kv, zc, g_out):
    s = z.shape[0]
    c = zc.shape[0]
    group = n_heads // n_kv
    gw = group * HEAD_DIM
    bw = n_heads * HEAD_DIM
    tq = min(ATTN_TQ, s)
    tk = min(ATTN_TK, s)
    assert q_col0 % gw == 0 and k_col0 % HEAD_DIM == 0 and v_col0 % HEAD_DIM == 0
    qb, kb, vb = q_col0 // gw, k_col0 // HEAD_DIM, v_col0 // HEAD_DIM
    return pl.pallas_call(
        functools.partial(_attn_kernel, group=group, tk=tk),
        grid=(s // tq, n_kv),
        in_specs=[pl.BlockSpec((tq, gw), lambda i, kh: (i, qb + kh)),
                  pl.BlockSpec((c, HEAD_DIM), lambda i, kh: (0, kh)),
                  pl.BlockSpec((c, HEAD_DIM), lambda i, kh: (0, n_kv + kh)),
                  pl.BlockSpec((s, HEAD_DIM), lambda i, kh: (0, kb + kh)),
                  pl.BlockSpec((s, HEAD_DIM), lambda i, kh: (0, vb + kh)),
                  pl.BlockSpec((1, bw), lambda i, kh: (0, 0))],
        out_specs=pl.BlockSpec((tq, bw), lambda i, kh: (i, 0)),
        out_shape=jax.ShapeDtypeStruct((s, bw), BF16),
        scratch_shapes=[pltpu.VMEM((group * tq, 1), F32),
                        pltpu.VMEM((group * tq, 1), F32),
                        pltpu.VMEM((group * tq, HEAD_DIM), F32),
                        pltpu.VMEM((n_heads, tq, HEAD_DIM), F32)],
        compiler_params=_params(2),
        name="attn",
    )(z, zc, zc, z, z, g_out.reshape(1, bw))


def _outproj_kernel(a_ref, b_ref, wa_ref, wb_ref, o_ref):
    o_ref[...] = (jnp.dot(a_ref[...], wa_ref[...], preferred_element_type=F32)
                  + jnp.dot(b_ref[...], wb_ref[...], preferred_element_type=F32))


def _outproj(a, b, w):
    s, aw = a.shape
    bw = b.shape[1]
    n = w.shape[1]
    tm = min(OUTPROJ_TM, s)
    tn = min(OUTPROJ_TN, n)
    assert aw == bw
    return pl.pallas_call(
        _outproj_kernel,
        grid=(n // tn, s // tm),
        in_specs=[pl.BlockSpec((tm, aw), lambda j, i: (i, 0)),
                  pl.BlockSpec((tm, bw), lambda j, i: (i, 0)),
                  pl.BlockSpec((aw, tn), lambda j, i: (0, j)),
                  pl.BlockSpec((bw, tn), lambda j, i: (1, j))],
        out_specs=pl.BlockSpec((tm, tn), lambda j, i: (i, j)),
        out_shape=jax.ShapeDtypeStruct((s, n), F32),
        compiler_params=_params(2),
        name="outproj",
    )(a, b, w, w)


def _post_kernel(x_ref, o_ref, gpost_ref, gate_ref, gpre_ref, sh_ref, sc_ref, xm_ref, h_ref):
    xm = x_ref[...] + gate_ref[0:1, :] * _rms(o_ref[...], gpost_ref[...])
    xm_ref[...] = xm
    h = _rms(xm, gpre_ref[...])
    h_ref[...] = (h * (1.0 + sc_ref[0:1, :]) + sh_ref[0:1, :]).astype(h_ref.dtype)


def _post(x, o, g_post, mods, g_pre):
    s, d = x.shape
    tm = min(ROW_TILE, s)
    row = pl.BlockSpec((tm, d), lambda i: (i, 0))
    vec = pl.BlockSpec((1, d), lambda i: (0, 0))

    def mod(idx):
        return pl.BlockSpec((2, d), lambda i: (0, idx))

    return pl.pallas_call(
        _post_kernel,
        grid=(s // tm,),
        in_specs=[row, row, vec, mod(2), vec, mod(3), mod(4)],
        out_specs=[row, row],
        out_shape=[jax.ShapeDtypeStruct((s, d), F32), jax.ShapeDtypeStruct((s, d), BF16)],
        compiler_params=_params(1),
        name="post",
    )(x, o, g_post.reshape(1, d), mods, g_pre.reshape(1, d), mods, mods)


def _ffn_kernel(hp_ref, hm_ref, hn_ref, wg_ref, wu_ref, cwg_ref, cwu_ref, cbg_ref, cbu_ref, wd_ref,
                o_ref, lhs_ref, r0_ref, r1_ref, act_ref, *, nf):
    i = pl.program_id(0)
    f = pl.program_id(1)
    tm = hm_ref.shape[0]
    halo = hp_ref.shape[0]
    rows = tm + 2 * halo
    r_refs = (r0_ref, r1_ref)

    def up_proj(r_ref, which):
        w_ref = (wg_ref, wu_ref)[which]
        r_ref[which] = jnp.dot(lhs_ref[...], w_ref[...], preferred_element_type=F32)

    def conv(win, cw_ref, cb_ref):
        n = win.shape[0]
        prev = pltpu.roll(win, 1, 0)[SUBLANES:n - SUBLANES, :]
        nxt = pltpu.roll(win, n - 1, 0)[SUBLANES:n - SUBLANES, :]
        return (prev * cw_ref[0:1, :] + win[SUBLANES:n - SUBLANES, :] * cw_ref[1:2, :]
                + nxt * cw_ref[2:3, :] + cb_ref[...])

    def activation(r_ref):
        rb = min(FFN_ACT_ROWS, tm)
        for b in range(tm // rb):
            lo = halo + b * rb - SUBLANES
            gate = conv(r_ref[0, lo:lo + rb + 2 * SUBLANES, :], cwg_ref, cbg_ref)
            up = conv(r_ref[1, lo:lo + rb + 2 * SUBLANES, :], cwu_ref, cbu_ref)
            act_ref[b * rb:(b + 1) * rb, :] = (gate * jax.nn.sigmoid(gate) * up).astype(BF16)

    def down_proj():
        o_ref[...] += jnp.dot(act_ref[...], wd_ref[...], preferred_element_type=F32)

    @pl.when(f == 0)
    def _():
        lhs_ref[0:halo, :] = jnp.where(i > 0, hp_ref[...], jnp.zeros_like(hp_ref[...]))
        lhs_ref[halo:halo + tm, :] = hm_ref[...]
        lhs_ref[halo + tm:, :] = jnp.where(i < pl.num_programs(0) - 1, hn_ref[...],
                                           jnp.zeros_like(hn_ref[...]))
        o_ref[...] = jnp.zeros(o_ref.shape, F32)
        up_proj(r0_ref, 0)
        up_proj(r0_ref, 1)

    for parity in (0, 1):
        @pl.when(jnp.logical_and(jnp.logical_and(f > 0, f < nf), f % 2 == parity))
        def _(parity=parity):
            activation(r_refs[1 - parity])
            up_proj(r_refs[parity], 0)
            down_proj()
            up_proj(r_refs[parity], 1)

    @pl.when(f == nf)
    def _():
        activation(r_refs[(nf - 1) % 2])
        down_proj()


def _ffn(h, w_gu, conv_w, conv_b, w_down):
    s, d = h.shape
    ff = w_down.shape[0]
    tm = min(FFN_TM, s)
    tf = min(FFN_TF, ff)
    assert ff % tf == 0 and s % tm == 0 and tm % BF16_ROWS == 0
    nf = ff // tf
    hb = tm // BF16_ROWS
    n_hb = s // BF16_ROWS
    single = pl.Buffered(1)

    def cur(f):
        return jnp.minimum(f, nf - 1)

    def prv(f):
        return jnp.maximum(f - 1, 0)

    return pl.pallas_call(
        functools.partial(_ffn_kernel, nf=nf),
        grid=(s // tm, nf + 1),
        in_specs=[pl.BlockSpec((BF16_ROWS, d), lambda i, f: (jnp.maximum(i * hb - 1, 0), 0)),
                  pl.BlockSpec((tm, d), lambda i, f: (i, 0), pipeline_mode=single),
                  pl.BlockSpec((BF16_ROWS, d), lambda i, f: (jnp.minimum((i + 1) * hb, n_hb - 1), 0)),
                  pl.BlockSpec((d, tf), lambda i, f: (0, cur(f))),
                  pl.BlockSpec((d, tf), lambda i, f: (0, nf + cur(f))),
                  pl.BlockSpec((3, tf), lambda i, f: (0, prv(f))),
                  pl.BlockSpec((3, tf), lambda i, f: (0, nf + prv(f))),
                  pl.BlockSpec((1, tf), lambda i, f: (0, prv(f))),
                  pl.BlockSpec((1, tf), lambda i, f: (0, nf + prv(f))),
                  pl.BlockSpec((tf, d), lambda i, f: (prv(f), 0))],
        out_specs=pl.BlockSpec((tm, d), lambda i, f: (i, 0), pipeline_mode=single),
        out_shape=jax.ShapeDtypeStruct((s, d), F32),
        scratch_shapes=[pltpu.VMEM((tm + 2 * BF16_ROWS, d), BF16),
                        pltpu.VMEM((2, tm + 2 * BF16_ROWS, tf), F32),
                        pltpu.VMEM((2, tm + 2 * BF16_ROWS, tf), F32),
                        pltpu.VMEM((tm, tf), BF16)],
        compiler_params=_params(2),
        name="ffn",
    )(h, h, h, w_gu, w_gu, conv_w, conv_w, conv_b, conv_b, w_down)


def _final_kernel(xm_ref, f_ref, g_ref, gate_ref, o_ref):
    o_ref[...] = xm_ref[...] + gate_ref[0:1, :] * _rms(f_ref[...], g_ref[...])


def _final(xm, f, g_post, mods):
    s, d = xm.shape
    tm = min(ROW_TILE, s)
    row = pl.BlockSpec((tm, d), lambda i: (i, 0))
    return pl.pallas_call(
        _final_kernel,
        grid=(s // tm,),
        in_specs=[row, row, pl.BlockSpec((1, d), lambda i: (0, 0)),
                  pl.BlockSpec((2, d), lambda i: (0, 5))],
        out_specs=row,
        out_shape=jax.ShapeDtypeStruct((s, d), F32),
        compiler_params=_params(1),
        name="final",
    )(xm, f, g_post.reshape(1, d), mods)


def _rope_tables(s):
    quarter = HEAD_DIM // 4
    pos = np.arange(s)
    freqs = ROPE_THETA ** (-np.arange(quarter, dtype=np.float64) / quarter)
    ang_r = (pos // GRID_W)[:, None] * freqs[None, :]
    ang_c = (pos % GRID_W)[:, None] * freqs[None, :]
    cos_t = np.concatenate([np.cos(ang_r), np.cos(ang_r), np.cos(ang_c), np.cos(ang_c)], axis=1)
    sin_t = np.concatenate([-np.sin(ang_r), np.sin(ang_r), -np.sin(ang_c), np.sin(ang_c)], axis=1)
    return jnp.asarray(cos_t, F32), jnp.asarray(sin_t, F32)


def kernel(x, c, ctx, c_ctx, w_ada, b_ada, g_pre_mix, g_post_mix, g_pre_ffn, g_post_ffn, w_in, ln_v_g,
           ln_v_b, w_s, b_s, g_q, g_k, g_out_a, g_out_b, w_out, w_up, conv_w, conv_b, w_down):
    bsz, s, d = x.shape
    assert bsz == 1 and w_in.shape[0] == 1, "single batch element, single layer"
    aw = ln_v_g.shape[-1]
    bw = g_out_b.shape[-1]
    in_cols = w_in.shape[-1]
    kvw = (in_cols - 2 * aw - bw) // 2
    n_heads, n_kv = bw // HEAD_DIM, kvw // HEAD_DIM
    q0, k0, v0 = 2 * aw, 2 * aw + bw, 2 * aw + bw + kvw
    ff = w_down.shape[1]

    x2 = x[0]
    ctx2 = ctx[0]
    w_in_b = w_in[0].astype(BF16)
    w_out_b = w_out[0].astype(BF16)
    w_gu_b = w_up[0].astype(BF16)
    w_down_b = w_down[0].astype(BF16)

    mods = _adaln(jnp.concatenate([c, c_ctx[None, :]], axis=0), w_ada[0], b_ada[0])

    cos_t, sin_t = _rope_tables(s)
    h = _prenorm(x2, g_pre_mix[0], mods, 0, 0, 1)
    hc = _prenorm(ctx2, g_pre_mix[0], mods, 1, 0, 1)

    kinds = (["gelu"] * (2 * aw // LANES) + ["q"] * n_heads + ["k"] * n_kv + ["v"] * n_kv)
    z = _inproj(h, w_in_b, 0, kinds, cos_t, sin_t, g_q[0], g_k[0])
    zc = _inproj(hc, w_in_b, k0, ["kc"] * n_kv + ["v"] * n_kv, cos_t, sin_t, g_q[0], g_k[0])

    out_a = _gmlp(z, 2 * aw, ln_v_g[0], ln_v_b[0], w_s[0], b_s[0], g_out_a[0])
    out_b = _attention(z, q0, k0, v0, n_heads, n_kv, zc, g_out_b[0])
    o = _outproj(out_a, out_b, w_out_b)
    x_mix, h2 = _post(x2, o, g_post_mix[0], mods, g_pre_ffn[0])
    f = _ffn(h2, w_gu_b, conv_w[0], conv_b[0].reshape(1, 2 * ff), w_down_b)
    out = _final(x_mix, f, g_post_ffn[0], mods)
    return out[None]
```

```python
import functools

import numpy as np
import jax
import jax.numpy as jnp
from jax import lax
from jax.experimental import pallas as pl
from jax.experimental.pallas import tpu as pltpu

GRID_W = 64
HEAD_DIM = 128
ROPE_THETA = 10000.0
EPS = 1e-6
N_MOD = 6
LOG2_E = 1.4426950408889634

LANES = 128
SUBLANES = 8
BF16_ROWS = 16
VMEM_LIMIT_BYTES = 56 * 1024 * 1024

ADALN_TN = 512
ROW_TILE = 256
INPROJ_TM = 512
INPROJ_TN = 1024
GMLP_CHUNKS_PER_STEP = 2
ATTN_TQ = 256
ATTN_TK = 1024
OUTPROJ_TM = 512
OUTPROJ_TN = 1024
FFN_TM = 1024
FFN_TF = 256
FFN_ACT_ROWS = 256

F32 = jnp.float32
BF16 = jnp.bfloat16


def _params(n_axes):
    return pltpu.CompilerParams(dimension_semantics=("arbitrary",) * n_axes,
                                vmem_limit_bytes=VMEM_LIMIT_BYTES)


def _rms(x, g):
    return x * lax.rsqrt(jnp.mean(x * x, axis=-1, keepdims=True) + EPS) * g


def _adaln_kernel(c_ref, w_ref, b_ref, o_ref, s_ref):
    k_dim, tn = w_ref.shape
    nj = tn // LANES

    @pl.when(pl.program_id(0) == 0)
    def _():
        c = c_ref[...]
        s_ref[...] = c * jax.nn.sigmoid(c)

    def body(r, accs):
        rows = pl.ds(pl.multiple_of(r * SUBLANES, SUBLANES), SUBLANES)
        s0 = s_ref[0, rows, :]
        s1 = s_ref[1, rows, :]
        new = []
        for j in range(nj):
            w = w_ref[rows, j * LANES:(j + 1) * LANES]
            new.append(accs[2 * j] + w * s0)
            new.append(accs[2 * j + 1] + w * s1)
        return tuple(new)

    zero = jnp.zeros((SUBLANES, LANES), F32)
    accs = lax.fori_loop(0, k_dim // SUBLANES, body, (zero,) * (2 * nj), unroll=4)
    for j in range(nj):
        cols = slice(j * LANES, (j + 1) * LANES)
        o_ref[0:1, cols] = jnp.sum(accs[2 * j], axis=0, keepdims=True) + b_ref[:, cols]
        o_ref[1:2, cols] = jnp.sum(accs[2 * j + 1], axis=0, keepdims=True) + b_ref[:, cols]


def _adaln(cc, w_ada, b_ada):
    k_dim, n = w_ada.shape
    tn = min(ADALN_TN, n)
    c_rep = jnp.broadcast_to(cc[:, :, None], (2, k_dim, LANES))
    return pl.pallas_call(
        _adaln_kernel,
        grid=(n // tn,),
        in_specs=[pl.BlockSpec((2, k_dim, LANES), lambda j: (0, 0, 0)),
                  pl.BlockSpec((k_dim, tn), lambda j: (0, j)),
                  pl.BlockSpec((1, tn), lambda j: (0, j))],
        out_specs=pl.BlockSpec((2, tn), lambda j: (0, j)),
        out_shape=jax.ShapeDtypeStruct((2, n), F32),
        scratch_shapes=[pltpu.VMEM((2, k_dim, LANES), F32)],
        compiler_params=_params(1),
        name="adaln",
    )(c_rep, w_ada, b_ada.reshape(1, n))


def _prenorm_kernel(x_ref, g_ref, sh_ref, sc_ref, o_ref, *, row):
    y = _rms(x_ref[...], g_ref[...])
    o_ref[...] = (y * (1.0 + sc_ref[row:row + 1, :]) + sh_ref[row:row + 1, :]).astype(o_ref.dtype)


def _prenorm(x, g, mods, row, shift_idx, scale_idx):
    s, d = x.shape
    tm = min(ROW_TILE, s)
    return pl.pallas_call(
        functools.partial(_prenorm_kernel, row=row),
        grid=(s // tm,),
        in_specs=[pl.BlockSpec((tm, d), lambda i: (i, 0)),
                  pl.BlockSpec((1, d), lambda i: (0, 0)),
                  pl.BlockSpec((2, d), lambda i: (0, shift_idx)),
                  pl.BlockSpec((2, d), lambda i: (0, scale_idx))],
        out_specs=pl.BlockSpec((tm, d), lambda i: (i, 0)),
        out_shape=jax.ShapeDtypeStruct((s, d), BF16),
        compiler_params=_params(1),
        name="prenorm",
    )(x, g.reshape(1, d), mods, mods)


def _swap_rope_halves(a):
    lane = lax.broadcasted_iota(jnp.int32, a.shape, 1)
    return jnp.where((lane & 32) != 0, pltpu.roll(a, 32, 1), pltpu.roll(a, LANES - 32, 1))


def _head_epilogue(a, kind, cos, sin, gq, gk):
    if kind == "gelu":
        return 0.5 * a * (1.0 + lax.erf(a * (2.0 ** -0.5)))
    if kind == "v":
        return a
    a = _rms(a, gq if kind == "q" else gk)
    if kind != "kc":
        a = a * cos + _swap_rope_halves(a) * sin
    if kind == "q":
        a = a * (HEAD_DIM ** -0.5 * LOG2_E)
    return a


def _inproj_kernel(h_ref, w_ref, cos_ref, sin_ref, gq_ref, gk_ref, o_ref, *, plans):
    j = pl.program_id(0)
    acc = jnp.dot(h_ref[...], w_ref[...], preferred_element_type=F32)
    for kinds, tiles in plans:
        cond = functools.reduce(jnp.logical_or, [j == t for t in tiles])

        @pl.when(cond)
        def _(kinds=kinds):
            for c, kind in enumerate(kinds):
                cols = slice(c * LANES, (c + 1) * LANES)
                o_ref[:, cols] = _head_epilogue(acc[:, cols], kind, cos_ref[...], sin_ref[...],
                                                gq_ref[...], gk_ref[...]).astype(o_ref.dtype)


def _inproj(h, w, col0, kinds_per_chunk, cos_t, sin_t, gq, gk):
    s, d = h.shape
    n = LANES * len(kinds_per_chunk)
    tm = min(INPROJ_TM, s)
    tn = min(INPROJ_TN, n)
    assert n % tn == 0 and col0 % tn == 0 and s % tm == 0
    per_tile = tn // LANES
    by_kinds = {}
    for t in range(n // tn):
        by_kinds.setdefault(tuple(kinds_per_chunk[t * per_tile:(t + 1) * per_tile]), []).append(t)
    plans = tuple((k, tuple(v)) for k, v in by_kinds.items())
    j0 = col0 // tn
    return pl.pallas_call(
        functools.partial(_inproj_kernel, plans=plans),
        grid=(n // tn, s // tm),
        in_specs=[pl.BlockSpec((tm, d), lambda j, i: (i, 0)),
                  pl.BlockSpec((d, tn), lambda j, i: (0, j + j0)),
                  pl.BlockSpec((tm, LANES), lambda j, i: (i, 0)),
                  pl.BlockSpec((tm, LANES), lambda j, i: (i, 0)),
                  pl.BlockSpec((1, LANES), lambda j, i: (0, 0)),
                  pl.BlockSpec((1, LANES), lambda j, i: (0, 0))],
        out_specs=pl.BlockSpec((tm, tn), lambda j, i: (i, j)),
        out_shape=jax.ShapeDtypeStruct((s, n), BF16),
        compiler_params=_params(2),
        name="inproj",
    )(h, w, cos_t, sin_t, gq.reshape(1, LANES), gk.reshape(1, LANES))


def _gmlp_kernel(z_ref, lng_ref, lnb_ref, ws_ref, bs_ref, go_ref, o_ref, oa_ref, *, chunk, groups):
    aw = groups * LANES
    for cidx in range(z_ref.shape[0] // chunk):
        rows = slice(cidx * chunk, (cidx + 1) * chunk)
        v = z_ref[rows, aw:].astype(F32)
        mu = jnp.mean(v, axis=-1, keepdims=True)
        vc = v - mu
        var = jnp.mean(vc * vc, axis=-1, keepdims=True)
        vn = (vc * lax.rsqrt(var + EPS) * lng_ref[...] + lnb_ref[...]).astype(BF16)
        for g in range(groups):
            cols = slice(g * LANES, (g + 1) * LANES)
            mixed = jnp.dot(ws_ref[g], vn[:, cols], preferred_element_type=F32) + bs_ref[g]
            oa_ref[:, cols] = z_ref[rows, cols].astype(F32) * mixed
        o_ref[rows, :] = _rms(oa_ref[...], go_ref[...]).astype(o_ref.dtype)


def _gmlp(z, n_cols_a, ln_g, ln_b, w_s, b_s, g_out):
    s = z.shape[0]
    groups, chunk, _ = w_s.shape
    aw = n_cols_a // 2
    assert aw == groups * LANES
    rows = chunk * min(GMLP_CHUNKS_PER_STEP, s // chunk)
    bs_rep = jnp.broadcast_to(b_s[:, :, None], (groups, chunk, LANES))
    return pl.pallas_call(
        functools.partial(_gmlp_kernel, chunk=chunk, groups=groups),
        grid=(s // rows,),
        in_specs=[pl.BlockSpec((rows, 2 * aw), lambda i: (i, 0)),
                  pl.BlockSpec((1, aw), lambda i: (0, 0)),
                  pl.BlockSpec((1, aw), lambda i: (0, 0)),
                  pl.BlockSpec((groups, chunk, chunk), lambda i: (0, 0, 0)),
                  pl.BlockSpec((groups, chunk, LANES), lambda i: (0, 0, 0)),
                  pl.BlockSpec((1, aw), lambda i: (0, 0))],
        out_specs=pl.BlockSpec((rows, aw), lambda i: (i, 0)),
        out_shape=jax.ShapeDtypeStruct((s, aw), BF16),
        scratch_shapes=[pltpu.VMEM((chunk, aw), F32)],
        compiler_params=_params(1),
        name="gmlp",
    )(z, ln_g.reshape(1, aw), ln_b.reshape(1, aw), w_s.astype(BF16), bs_rep, g_out.reshape(1, aw))


def _attn_kernel(q_ref, kc_ref, vc_ref, k_ref, v_ref, go_ref, o_ref, ob_ref, *, group, tk):
    kh = pl.program_id(1)
    n_kh = pl.num_programs(1)

    chunks = [(kc_ref, 0, kc_ref.shape[0])]
    chunks += [(k_ref, c * tk, tk) for c in range(k_ref.shape[0] // tk)]
    v_refs = {id(kc_ref): vc_ref, id(k_ref): v_ref}
    tiles = [(ci, g) for ci in range(len(chunks)) for g in range(group)]

    def scores(tile):
        ci, g = tile
        ref, lo, n = chunks[ci]
        q = q_ref[:, g * HEAD_DIM:(g + 1) * HEAD_DIM]
        return lax.dot_general(q, ref[lo:lo + n, :], (((1,), (1,)), ((), ())),
                               preferred_element_type=F32)

    state = [None] * group

    def consume(tile, s):
        ci, g = tile
        ref, lo, n = chunks[ci]
        v = v_refs[id(ref)][lo:lo + n, :]
        m_blk = jnp.max(s, axis=-1, keepdims=True)
        if state[g] is None:
            p = jnp.exp2(s - m_blk)
            state[g] = (m_blk, jnp.sum(p, axis=-1, keepdims=True),
                        jnp.dot(p.astype(BF16), v, preferred_element_type=F32))
            return
        m_old, l_old, acc = state[g]
        m_new = jnp.maximum(m_old, m_blk)
        p = jnp.exp2(s - m_new)
        alpha = jnp.exp2(m_old - m_new)
        state[g] = (m_new, alpha * l_old + jnp.sum(p, axis=-1, keepdims=True),
                    alpha * acc + jnp.dot(p.astype(BF16), v, preferred_element_type=F32))

    s_next = scores(tiles[0])
    for t, tile in enumerate(tiles):
        s_cur = s_next
        if t + 1 < len(tiles):
            s_next = scores(tiles[t + 1])
        consume(tile, s_cur)

    for g in range(group):
        _, l, acc = state[g]
        ob_ref[kh * group + g] = acc / l

    @pl.when(kh == n_kh - 1)
    def _():
        n_heads = ob_ref.shape[0]
        ssq = jnp.zeros((q_ref.shape[0], 1), F32)
        for h in range(n_heads):
            o = ob_ref[h]
            ssq = ssq + jnp.sum(o * o, axis=-1, keepdims=True)
        inv = lax.rsqrt(ssq / (n_heads * HEAD_DIM) + EPS)
        for h in range(n_heads):
            cols = slice(h * HEAD_DIM, (h + 1) * HEAD_DIM)
            o_ref[:, cols] = (ob_ref[h] * inv * go_ref[:, cols]).astype(o_ref.dtype)


def _attention(z, q_col0, k_col0, v_col0, n_heads, n_kv, zc, g_out):
    s = z.shape[0]
    c = zc.shape[0]
    group = n_heads // n_kv
    gw = group * HEAD_DIM
    bw = n_heads * HEAD_DIM
    tq = min(ATTN_TQ, s)
    tk = min(ATTN_TK, s)
    assert q_col0 % gw == 0 and k_col0 % HEAD_DIM == 0 and v_col0 % HEAD_DIM == 0
    qb, kb, vb = q_col0 // gw, k_col0 // HEAD_DIM, v_col0 // HEAD_DIM
    return pl.pallas_call(
        functools.partial(_attn_kernel, group=group, tk=tk),
        grid=(s // tq, n_kv),
        in_specs=[pl.BlockSpec((tq, gw), lambda i, kh: (i, qb + kh)),
                  pl.BlockSpec((c, HEAD_DIM), lambda i, kh: (0, kh)),
                  pl.BlockSpec((c, HEAD_DIM), lambda i, kh: (0, n_kv + kh)),
                  pl.BlockSpec((s, HEAD_DIM), lambda i, kh: (0, kb + kh)),
                  pl.BlockSpec((s, HEAD_DIM), lambda i, kh: (0, vb + kh)),
                  pl.BlockSpec((1, bw), lambda i, kh: (0, 0))],
        out_specs=pl.BlockSpec((tq, bw), lambda i, kh: (i, 0)),
        out_shape=jax.ShapeDtypeStruct((s, bw), BF16),
        scratch_shapes=[pltpu.VMEM((n_heads, tq, HEAD_DIM), F32)],
        compiler_params=_params(2),
        name="attn",
    )(z, zc, zc, z, z, g_out.reshape(1, bw))


def _outproj_kernel(a_ref, b_ref, wa_ref, wb_ref, o_ref):
    o_ref[...] = (jnp.dot(a_ref[...], wa_ref[...], preferred_element_type=F32)
                  + jnp.dot(b_ref[...], wb_ref[...], preferred_element_type=F32))


def _outproj(a, b, w):
    s, aw = a.shape
    bw = b.shape[1]
    n = w.shape[1]
    tm = min(OUTPROJ_TM, s)
    tn = min(OUTPROJ_TN, n)
    assert aw == bw
    return pl.pallas_call(
        _outproj_kernel,
        grid=(n // tn, s // tm),
        in_specs=[pl.BlockSpec((tm, aw), lambda j, i: (i, 0)),
                  pl.BlockSpec((tm, bw), lambda j, i: (i, 0)),
                  pl.BlockSpec((aw, tn), lambda j, i: (0, j)),
                  pl.BlockSpec((bw, tn), lambda j, i: (1, j))],
        out_specs=pl.BlockSpec((tm, tn), lambda j, i: (i, j)),
        out_shape=jax.ShapeDtypeStruct((s, n), F32),
        compiler_params=_params(2),
        name="outproj",
    )(a, b, w, w)


def _post_kernel(x_ref, o_ref, gpost_ref, gate_ref, gpre_ref, sh_ref, sc_ref, xm_ref, h_ref):
    xm = x_ref[...] + gate_ref[0:1, :] * _rms(o_ref[...], gpost_ref[...])
    xm_ref[...] = xm
    h = _rms(xm, gpre_ref[...])
    h_ref[...] = (h * (1.0 + sc_ref[0:1, :]) + sh_ref[0:1, :]).astype(h_ref.dtype)


def _post(x, o, g_post, mods, g_pre):
    s, d = x.shape
    tm = min(ROW_TILE, s)
    row = pl.BlockSpec((tm, d), lambda i: (i, 0))
    vec = pl.BlockSpec((1, d), lambda i: (0, 0))

    def mod(idx):
        return pl.BlockSpec((2, d), lambda i: (0, idx))

    return pl.pallas_call(
        _post_kernel,
        grid=(s // tm,),
        in_specs=[row, row, vec, mod(2), vec, mod(3), mod(4)],
        out_specs=[row, row],
        out_shape=[jax.ShapeDtypeStruct((s, d), F32), jax.ShapeDtypeStruct((s, d), BF16)],
        compiler_params=_params(1),
        name="post",
    )(x, o, g_post.reshape(1, d), mods, g_pre.reshape(1, d), mods, mods)


def _ffn_kernel(hp_ref, hm_ref, hn_ref, wg_ref, wu_ref, cwg_ref, cwu_ref, cbg_ref, cbu_ref, wd_ref,
                o_ref, lhs_ref, r0_ref, r1_ref, act_ref, *, nf):
    i = pl.program_id(0)
    f = pl.program_id(1)
    tm = hm_ref.shape[0]
    halo = hp_ref.shape[0]
    rows = tm + 2 * halo
    r_refs = (r0_ref, r1_ref)

    def up_proj(r_ref, which):
        w_ref = (wg_ref, wu_ref)[which]
        r_ref[which] = jnp.dot(lhs_ref[...], w_ref[...], preferred_element_type=F32)

    def conv(win, cw_ref, cb_ref):
        n = win.shape[0]
        prev = pltpu.roll(win, 1, 0)[SUBLANES:n - SUBLANES, :]
        nxt = pltpu.roll(win, n - 1, 0)[SUBLANES:n - SUBLANES, :]
        return (prev * cw_ref[0:1, :] + win[SUBLANES:n - SUBLANES, :] * cw_ref[1:2, :]
                + nxt * cw_ref[2:3, :] + cb_ref[...])

    def activation(r_ref):
        rb = min(FFN_ACT_ROWS, tm)
        for b in range(tm // rb):
            lo = halo + b * rb - SUBLANES
            gate = conv(r_ref[0, lo:lo + rb + 2 * SUBLANES, :], cwg_ref, cbg_ref)
            up = conv(r_ref[1, lo:lo + rb + 2 * SUBLANES, :], cwu_ref, cbu_ref)
            act_ref[b * rb:(b + 1) * rb, :] = (gate * jax.nn.sigmoid(gate) * up).astype(BF16)

    def down_proj():
        o_ref[...] += jnp.dot(act_ref[...], wd_ref[...], preferred_element_type=F32)

    @pl.when(f == 0)
    def _():
        lhs_ref[0:halo, :] = jnp.where(i > 0, hp_ref[...], jnp.zeros_like(hp_ref[...]))
        lhs_ref[halo:halo + tm, :] = hm_ref[...]
        lhs_ref[halo + tm:, :] = jnp.where(i < pl.num_programs(0) - 1, hn_ref[...],
                                           jnp.zeros_like(hn_ref[...]))
        o_ref[...] = jnp.zeros(o_ref.shape, F32)
        up_proj(r0_ref, 0)
        up_proj(r0_ref, 1)

    for parity in (0, 1):
        @pl.when(jnp.logical_and(jnp.logical_and(f > 0, f < nf), f % 2 == parity))
        def _(parity=parity):
            activation(r_refs[1 - parity])
            up_proj(r_refs[parity], 0)
            down_proj()
            up_proj(r_refs[parity], 1)

    @pl.when(f == nf)
    def _():
        activation(r_refs[(nf - 1) % 2])
        down_proj()


def _ffn(h, w_gu, conv_w, conv_b, w_down):
    s, d = h.shape
    ff = w_down.shape[0]
    tm = min(FFN_TM, s)
    tf = min(FFN_TF, ff)
    assert ff % tf == 0 and s % tm == 0 and tm % BF16_ROWS == 0
    nf = ff // tf
    hb = tm // BF16_ROWS
    n_hb = s // BF16_ROWS
    single = pl.Buffered(1)

    def cur(f):
        return jnp.minimum(f, nf - 1)

    def prv(f):
        return jnp.maximum(f - 1, 0)

    return pl.pallas_call(
        functools.partial(_ffn_kernel, nf=nf),
        grid=(s // tm, nf + 1),
        in_specs=[pl.BlockSpec((BF16_ROWS, d), lambda i, f: (jnp.maximum(i * hb - 1, 0), 0)),
                  pl.BlockSpec((tm, d), lambda i, f: (i, 0), pipeline_mode=single),
                  pl.BlockSpec((BF16_ROWS, d), lambda i, f: (jnp.minimum((i + 1) * hb, n_hb - 1), 0)),
                  pl.BlockSpec((d, tf), lambda i, f: (0, cur(f))),
                  pl.BlockSpec((d, tf), lambda i, f: (0, nf + cur(f))),
                  pl.BlockSpec((3, tf), lambda i, f: (0, prv(f))),
                  pl.BlockSpec((3, tf), lambda i, f: (0, nf + prv(f))),
                  pl.BlockSpec((1, tf), lambda i, f: (0, prv(f))),
                  pl.BlockSpec((1, tf), lambda i, f: (0, nf + prv(f))),
                  pl.BlockSpec((tf, d), lambda i, f: (prv(f), 0))],
        out_specs=pl.BlockSpec((tm, d), lambda i, f: (i, 0), pipeline_mode=single),
        out_shape=jax.ShapeDtypeStruct((s, d), F32),
        scratch_shapes=[pltpu.VMEM((tm + 2 * BF16_ROWS, d), BF16),
                        pltpu.VMEM((2, tm + 2 * BF16_ROWS, tf), F32),
                        pltpu.VMEM((2, tm + 2 * BF16_ROWS, tf), F32),
                        pltpu.VMEM((tm, tf), BF16)],
        compiler_params=_params(2),
        name="ffn",
    )(h, h, h, w_gu, w_gu, conv_w, conv_w, conv_b, conv_b, w_down)


def _final_kernel(xm_ref, f_ref, g_ref, gate_ref, o_ref):
    o_ref[...] = xm_ref[...] + gate_ref[0:1, :] * _rms(f_ref[...], g_ref[...])


def _final(xm, f, g_post, mods):
    s, d = xm.shape
    tm = min(ROW_TILE, s)
    row = pl.BlockSpec((tm, d), lambda i: (i, 0))
    return pl.pallas_call(
        _final_kernel,
        grid=(s // tm,),
        in_specs=[row, row, pl.BlockSpec((1, d), lambda i: (0, 0)),
                  pl.BlockSpec((2, d), lambda i: (0, 5))],
        out_specs=row,
        out_shape=jax.ShapeDtypeStruct((s, d), F32),
        compiler_params=_params(1),
        name="final",
    )(xm, f, g_post.reshape(1, d), mods)


def _rope_tables(s):
    quarter = HEAD_DIM // 4
    pos = np.arange(s)
    freqs = ROPE_THETA ** (-np.arange(quarter, dtype=np.float64) / quarter)
    ang_r = (pos // GRID_W)[:, None] * freqs[None, :]
    ang_c = (pos % GRID_W)[:, None] * freqs[None, :]
    cos_t = np.concatenate([np.cos(ang_r), np.cos(ang_r), np.cos(ang_c), np.cos(ang_c)], axis=1)
    sin_t = np.concatenate([-np.sin(ang_r), np.sin(ang_r), -np.sin(ang_c), np.sin(ang_c)], axis=1)
    return jnp.asarray(cos_t, F32), jnp.asarray(sin_t, F32)


def kernel(x, c, ctx, c_ctx, w_ada, b_ada, g_pre_mix, g_post_mix, g_pre_ffn, g_post_ffn, w_in, ln_v_g,
           ln_v_b, w_s, b_s, g_q, g_k, g_out_a, g_out_b, w_out, w_up, conv_w, conv_b, w_down):
    bsz, s, d = x.shape
    assert bsz == 1 and w_in.shape[0] == 1, "single batch element, single layer"
    aw = ln_v_g.shape[-1]
    bw = g_out_b.shape[-1]
    in_cols = w_in.shape[-1]
    kvw = (in_cols - 2 * aw - bw) // 2
    n_heads, n_kv = bw // HEAD_DIM, kvw // HEAD_DIM
    q0, k0, v0 = 2 * aw, 2 * aw + bw, 2 * aw + bw + kvw
    ff = w_down.shape[1]

    x2 = x[0]
    ctx2 = ctx[0]
    w_in_b = w_in[0].astype(BF16)
    w_out_b = w_out[0].astype(BF16)
    w_gu_b = w_up[0].astype(BF16)
    w_down_b = w_down[0].astype(BF16)

    mods = _adaln(jnp.concatenate([c, c_ctx[None, :]], axis=0), w_ada[0], b_ada[0])

    cos_t, sin_t = _rope_tables(s)
    h = _prenorm(x2, g_pre_mix[0], mods, 0, 0, 1)
    hc = _prenorm(ctx2, g_pre_mix[0], mods, 1, 0, 1)

    kinds = (["gelu"] * (2 * aw // LANES) + ["q"] * n_heads + ["k"] * n_kv + ["v"] * n_kv)
    z = _inproj(h, w_in_b, 0, kinds, cos_t, sin_t, g_q[0], g_k[0])
    zc = _inproj(hc, w_in_b, k0, ["kc"] * n_kv + ["v"] * n_kv, cos_t, sin_t, g_q[0], g_k[0])

    out_a = _gmlp(z, 2 * aw, ln_v_g[0], ln_v_b[0], w_s[0], b_s[0], g_out_a[0])
    out_b = _attention(z, q0, k0, v0, n_heads, n_kv, zc, g_out_b[0])
    o = _outproj(out_a, out_b, w_out_b)
    x_mix, h2 = _post(x2, o, g_post_mix[0], mods, g_pre_ffn[0])
    f = _ffn(h2, w_gu_b, conv_w[0], conv_b[0].reshape(1, 2 * ff), w_down_b)
    out = _final(x_mix, f, g_post_ffn[0], mods)
    return out[None]
```

```python
import functools

import numpy as np
import jax
import jax.numpy as jnp
from jax import lax
from jax.experimental import pallas as pl
from jax.experimental.pallas import tpu as pltpu

GRID_W = 64
HEAD_DIM = 128
ROPE_THETA = 10000.0
EPS = 1e-6
N_MOD = 6
LOG2_E = 1.4426950408889634

LANES = 128
SUBLANES = 8
BF16_ROWS = 16
VMEM_LIMIT_BYTES = 56 * 1024 * 1024

ADALN_TN = 512
ROW_TILE = 256
INPROJ_TM = 512
INPROJ_TN = 1024
GMLP_CHUNKS_PER_STEP = 2
ATTN_TQ = 256
ATTN_TK = 1024
OUTPROJ_TM = 512
OUTPROJ_TN = 1024
FFN_TM = 1024
FFN_TF = 256
FFN_ACT_ROWS = 256

F32 = jnp.float32
BF16 = jnp.bfloat16


def _params(n_axes):
    return pltpu.CompilerParams(dimension_semantics=("arbitrary",) * n_axes,
                                vmem_limit_bytes=VMEM_LIMIT_BYTES)


def _rms(x, g):
    return x * lax.rsqrt(jnp.mean(x * x, axis=-1, keepdims=True) + EPS) * g


def _adaln_kernel(c_ref, w_ref, b_ref, o_ref, s_ref):
    k_dim, tn = w_ref.shape
    nj = tn // LANES

    @pl.when(pl.program_id(0) == 0)
    def _():
        c = c_ref[...]
        s_ref[...] = c * jax.nn.sigmoid(c)

    def body(r, accs):
        rows = pl.ds(pl.multiple_of(r * SUBLANES, SUBLANES), SUBLANES)
        s0 = s_ref[0, rows, :]
        s1 = s_ref[1, rows, :]
        new = []
        for j in range(nj):
            w = w_ref[rows, j * LANES:(j + 1) * LANES]
            new.append(accs[2 * j] + w * s0)
            new.append(accs[2 * j + 1] + w * s1)
        return tuple(new)

    zero = jnp.zeros((SUBLANES, LANES), F32)
    accs = lax.fori_loop(0, k_dim // SUBLANES, body, (zero,) * (2 * nj), unroll=4)
    for j in range(nj):
        cols = slice(j * LANES, (j + 1) * LANES)
        o_ref[0:1, cols] = jnp.sum(accs[2 * j], axis=0, keepdims=True) + b_ref[:, cols]
        o_ref[1:2, cols] = jnp.sum(accs[2 * j + 1], axis=0, keepdims=True) + b_ref[:, cols]


def _adaln(cc, w_ada, b_ada):
    k_dim, n = w_ada.shape
    tn = min(ADALN_TN, n)
    c_rep = jnp.broadcast_to(cc[:, :, None], (2, k_dim, LANES))
    return pl.pallas_call(
        _adaln_kernel,
        grid=(n // tn,),
        in_specs=[pl.BlockSpec((2, k_dim, LANES), lambda j: (0, 0, 0)),
                  pl.BlockSpec((k_dim, tn), lambda j: (0, j)),
                  pl.BlockSpec((1, tn), lambda j: (0, j))],
        out_specs=pl.BlockSpec((2, tn), lambda j: (0, j)),
        out_shape=jax.ShapeDtypeStruct((2, n), F32),
        scratch_shapes=[pltpu.VMEM((2, k_dim, LANES), F32)],
        compiler_params=_params(1),
        name="adaln",
    )(c_rep, w_ada, b_ada.reshape(1, n))


def _prenorm_kernel(x_ref, g_ref, sh_ref, sc_ref, o_ref, *, row):
    y = _rms(x_ref[...], g_ref[...])
    o_ref[...] = (y * (1.0 + sc_ref[row:row + 1, :]) + sh_ref[row:row + 1, :]).astype(o_ref.dtype)


def _prenorm(x, g, mods, row, shift_idx, scale_idx):
    s, d = x.shape
    tm = min(ROW_TILE, s)
    return pl.pallas_call(
        functools.partial(_prenorm_kernel, row=row),
        grid=(s // tm,),
        in_specs=[pl.BlockSpec((tm, d), lambda i: (i, 0)),
                  pl.BlockSpec((1, d), lambda i: (0, 0)),
                  pl.BlockSpec((2, d), lambda i: (0, shift_idx)),
                  pl.BlockSpec((2, d), lambda i: (0, scale_idx))],
        out_specs=pl.BlockSpec((tm, d), lambda i: (i, 0)),
        out_shape=jax.ShapeDtypeStruct((s, d), BF16),
        compiler_params=_params(1),
        name="prenorm",
    )(x, g.reshape(1, d), mods, mods)


def _swap_rope_halves(a):
    lane = lax.broadcasted_iota(jnp.int32, a.shape, 1)
    return jnp.where((lane & 32) != 0, pltpu.roll(a, 32, 1), pltpu.roll(a, LANES - 32, 1))


def _head_epilogue(a, kind, cos, sin, gq, gk):
    if kind == "gelu":
        return 0.5 * a * (1.0 + lax.erf(a * (2.0 ** -0.5)))
    if kind == "v":
        return a
    a = _rms(a, gq if kind == "q" else gk)
    if kind != "kc":
        a = a * cos + _swap_rope_halves(a) * sin
    if kind == "q":
        a = a * (HEAD_DIM ** -0.5 * LOG2_E)
    return a


def _inproj_kernel(h_ref, w_ref, cos_ref, sin_ref, gq_ref, gk_ref, o_ref, *, plans):
    j = pl.program_id(0)
    acc = jnp.dot(h_ref[...], w_ref[...], preferred_element_type=F32)
    for kinds, tiles in plans:
        cond = functools.reduce(jnp.logical_or, [j == t for t in tiles])

        @pl.when(cond)
        def _(kinds=kinds):
            for c, kind in enumerate(kinds):
                cols = slice(c * LANES, (c + 1) * LANES)
                o_ref[:, cols] = _head_epilogue(acc[:, cols], kind, cos_ref[...], sin_ref[...],
                                                gq_ref[...], gk_ref[...]).astype(o_ref.dtype)


def _inproj(h, w, col0, kinds_per_chunk, cos_t, sin_t, gq, gk):
    s, d = h.shape
    n = LANES * len(kinds_per_chunk)
    tm = min(INPROJ_TM, s)
    tn = min(INPROJ_TN, n)
    assert n % tn == 0 and col0 % tn == 0 and s % tm == 0
    per_tile = tn // LANES
    by_kinds = {}
    for t in range(n // tn):
        by_kinds.setdefault(tuple(kinds_per_chunk[t * per_tile:(t + 1) * per_tile]), []).append(t)
    plans = tuple((k, tuple(v)) for k, v in by_kinds.items())
    j0 = col0 // tn
    return pl.pallas_call(
        functools.partial(_inproj_kernel, plans=plans),
        grid=(n // tn, s // tm),
        in_specs=[pl.BlockSpec((tm, d), lambda j, i: (i, 0)),
                  pl.BlockSpec((d, tn), lambda j, i: (0, j + j0)),
                  pl.BlockSpec((tm, LANES), lambda j, i: (i, 0)),
                  pl.BlockSpec((tm, LANES), lambda j, i: (i, 0)),
                  pl.BlockSpec((1, LANES), lambda j, i: (0, 0)),
                  pl.BlockSpec((1, LANES), lambda j, i: (0, 0))],
        out_specs=pl.BlockSpec((tm, tn), lambda j, i: (i, j)),
        out_shape=jax.ShapeDtypeStruct((s, n), BF16),
        compiler_params=_params(2),
        name="inproj",
    )(h, w, cos_t, sin_t, gq.reshape(1, LANES), gk.reshape(1, LANES))


def _gmlp_kernel(z_ref, lng_ref, lnb_ref, ws_ref, bs_ref, go_ref, o_ref, oa_ref, *, chunk, groups):
    aw = groups * LANES
    for cidx in range(z_ref.shape[0] // chunk):
        rows = slice(cidx * chunk, (cidx + 1) * chunk)
        v = z_ref[rows, aw:].astype(F32)
        mu = jnp.mean(v, axis=-1, keepdims=True)
        vc = v - mu
        var = jnp.mean(vc * vc, axis=-1, keepdims=True)
        vn = (vc * lax.rsqrt(var + EPS) * lng_ref[...] + lnb_ref[...]).astype(BF16)
        for g in range(groups):
            cols = slice(g * LANES, (g + 1) * LANES)
            mixed = jnp.dot(ws_ref[g], vn[:, cols], preferred_element_type=F32) + bs_ref[g]
            oa_ref[:, cols] = z_ref[rows, cols].astype(F32) * mixed
        o_ref[rows, :] = _rms(oa_ref[...], go_ref[...]).astype(o_ref.dtype)


def _gmlp(z, n_cols_a, ln_g, ln_b, w_s, b_s, g_out):
    s = z.shape[0]
    groups, chunk, _ = w_s.shape
    aw = n_cols_a // 2
    assert aw == groups * LANES
    rows = chunk * min(GMLP_CHUNKS_PER_STEP, s // chunk)
    bs_rep = jnp.broadcast_to(b_s[:, :, None], (groups, chunk, LANES))
    return pl.pallas_call(
        functools.partial(_gmlp_kernel, chunk=chunk, groups=groups),
        grid=(s // rows,),
        in_specs=[pl.BlockSpec((rows, 2 * aw), lambda i: (i, 0)),
                  pl.BlockSpec((1, aw), lambda i: (0, 0)),
                  pl.BlockSpec((1, aw), lambda i: (0, 0)),
                  pl.BlockSpec((groups, chunk, chunk), lambda i: (0, 0, 0)),
                  pl.BlockSpec((groups, chunk, LANES), lambda i: (0, 0, 0)),
                  pl.BlockSpec((1, aw), lambda i: (0, 0))],
        out_specs=pl.BlockSpec((rows, aw), lambda i: (i, 0)),
        out_shape=jax.ShapeDtypeStruct((s, aw), BF16),
        scratch_shapes=[pltpu.VMEM((chunk, aw), F32)],
        compiler_params=_params(1),
        name="gmlp",
    )(z, ln_g.reshape(1, aw), ln_b.reshape(1, aw), w_s.astype(BF16), bs_rep, g_out.reshape(1, aw))


def _attn_kernel(q_ref, kc_ref, vc_ref, k_ref, v_ref, go_ref, o_ref, ob_ref, *, group, tk):
    kh = pl.program_id(1)
    n_kh = pl.num_programs(1)

    chunks = [(kc_ref, 0, kc_ref.shape[0])]
    chunks += [(k_ref, c * tk, tk) for c in range(k_ref.shape[0] // tk)]
    v_refs = {id(kc_ref): vc_ref, id(k_ref): v_ref}
    tiles = [(ci, g) for ci in range(len(chunks)) for g in range(group)]

    def scores(tile):
        ci, g = tile
        ref, lo, n = chunks[ci]
        q = q_ref[:, g * HEAD_DIM:(g + 1) * HEAD_DIM]
        return lax.dot_general(q, ref[lo:lo + n, :], (((1,), (1,)), ((), ())),
                               preferred_element_type=F32)

    stats = [None] * group
    accs = [None] * group

    def softmax(tile, s):
        g = tile[1]
        m_blk = jnp.max(s, axis=-1, keepdims=True)
        if stats[g] is None:
            p = jnp.exp2(s - m_blk)
            stats[g] = (m_blk, jnp.sum(p, axis=-1, keepdims=True))
            return tile, p.astype(BF16), None
        m_old, l_old = stats[g]
        m_new = jnp.maximum(m_old, m_blk)
        p = jnp.exp2(s - m_new)
        alpha = jnp.exp2(m_old - m_new)
        stats[g] = (m_new, alpha * l_old + jnp.sum(p, axis=-1, keepdims=True))
        return tile, p.astype(BF16), alpha

    def values(pending):
        (ci, g), p, alpha = pending
        ref, lo, n = chunks[ci]
        pv = jnp.dot(p, v_refs[id(ref)][lo:lo + n, :], preferred_element_type=F32)
        accs[g] = pv if alpha is None else alpha * accs[g] + pv

    s_next = scores(tiles[0])
    pending = None
    for t, tile in enumerate(tiles):
        s_cur = s_next
        if t + 1 < len(tiles):
            s_next = scores(tiles[t + 1])
        done = softmax(tile, s_cur)
        if pending is not None:
            values(pending)
        pending = done
    values(pending)

    for g in range(group):
        ob_ref[kh * group + g] = accs[g] / stats[g][1]

    @pl.when(kh == n_kh - 1)
    def _():
        n_heads = ob_ref.shape[0]
        ssq = jnp.zeros((q_ref.shape[0], 1), F32)
        for h in range(n_heads):
            o = ob_ref[h]
            ssq = ssq + jnp.sum(o * o, axis=-1, keepdims=True)
        inv = lax.rsqrt(ssq / (n_heads * HEAD_DIM) + EPS)
        for h in range(n_heads):
            cols = slice(h * HEAD_DIM, (h + 1) * HEAD_DIM)
            o_ref[:, cols] = (ob_ref[h] * inv * go_ref[:, cols]).astype(o_ref.dtype)


def _attention(z, q_col0, k_col0, v_col0, n_heads, n_kv, zc, g_out):
    s = z.shape[0]
    c = zc.shape[0]
    group = n_heads // n_kv
    gw = group * HEAD_DIM
    bw = n_heads * HEAD_DIM
    tq = min(ATTN_TQ, s)
    tk = min(ATTN_TK, s)
    assert q_col0 % gw == 0 and k_col0 % HEAD_DIM == 0 and v_col0 % HEAD_DIM == 0
    qb, kb, vb = q_col0 // gw, k_col0 // HEAD_DIM, v_col0 // HEAD_DIM
    return pl.pallas_call(
        functools.partial(_attn_kernel, group=group, tk=tk),
        grid=(s // tq, n_kv),
        in_specs=[pl.BlockSpec((tq, gw), lambda i, kh: (i, qb + kh)),
                  pl.BlockSpec((c, HEAD_DIM), lambda i, kh: (0, kh)),
                  pl.BlockSpec((c, HEAD_DIM), lambda i, kh: (0, n_kv + kh)),
                  pl.BlockSpec((s, HEAD_DIM), lambda i, kh: (0, kb + kh)),
                  pl.BlockSpec((s, HEAD_DIM), lambda i, kh: (0, vb + kh)),
                  pl.BlockSpec((1, bw), lambda i, kh: (0, 0))],
        out_specs=pl.BlockSpec((tq, bw), lambda i, kh: (i, 0)),
        out_shape=jax.ShapeDtypeStruct((s, bw), BF16),
        scratch_shapes=[pltpu.VMEM((n_heads, tq, HEAD_DIM), F32)],
        compiler_params=_params(2),
        name="attn",
    )(z, zc, zc, z, z, g_out.reshape(1, bw))


def _outproj_kernel(a_ref, b_ref, wa_ref, wb_ref, o_ref):
    o_ref[...] = (jnp.dot(a_ref[...], wa_ref[...], preferred_element_type=F32)
                  + jnp.dot(b_ref[...], wb_ref[...], preferred_element_type=F32))


def _outproj(a, b, w):
    s, aw = a.shape
    bw = b.shape[1]
    n = w.shape[1]
    tm = min(OUTPROJ_TM, s)
    tn = min(OUTPROJ_TN, n)
    assert aw == bw
    return pl.pallas_call(
        _outproj_kernel,
        grid=(n // tn, s // tm),
        in_specs=[pl.BlockSpec((tm, aw), lambda j, i: (i, 0)),
                  pl.BlockSpec((tm, bw), lambda j, i: (i, 0)),
                  pl.BlockSpec((aw, tn), lambda j, i: (0, j)),
                  pl.BlockSpec((bw, tn), lambda j, i: (1, j))],
        out_specs=pl.BlockSpec((tm, tn), lambda j, i: (i, j)),
        out_shape=jax.ShapeDtypeStruct((s, n), F32),
        compiler_params=_params(2),
        name="outproj",
    )(a, b, w, w)


def _post_kernel(x_ref, o_ref, gpost_ref, gate_ref, gpre_ref, sh_ref, sc_ref, xm_ref, h_ref):
    xm = x_ref[...] + gate_ref[0:1, :] * _rms(o_ref[...], gpost_ref[...])
    xm_ref[...] = xm
    h = _rms(xm, gpre_ref[...])
    h_ref[...] = (h * (1.0 + sc_ref[0:1, :]) + sh_ref[0:1, :]).astype(h_ref.dtype)


def _post(x, o, g_post, mods, g_pre):
    s, d = x.shape
    tm = min(ROW_TILE, s)
    row = pl.BlockSpec((tm, d), lambda i: (i, 0))
    vec = pl.BlockSpec((1, d), lambda i: (0, 0))

    def mod(idx):
        return pl.BlockSpec((2, d), lambda i: (0, idx))

    return pl.pallas_call(
        _post_kernel,
        grid=(s // tm,),
        in_specs=[row, row, vec, mod(2), vec, mod(3), mod(4)],
        out_specs=[row, row],
        out_shape=[jax.ShapeDtypeStruct((s, d), F32), jax.ShapeDtypeStruct((s, d), BF16)],
        compiler_params=_params(1),
        name="post",
    )(x, o, g_post.reshape(1, d), mods, g_pre.reshape(1, d), mods, mods)


def _ffn_kernel(hp_ref, hm_ref, hn_ref, wg_ref, wu_ref, cwg_ref, cwu_ref, cbg_ref, cbu_ref, wd_ref,
                o_ref, lhs_ref, r0_ref, r1_ref, *, nf):
    i = pl.program_id(0)
    f = pl.program_id(1)
    tm = hm_ref.shape[0]
    halo = hp_ref.shape[0]
    rows = tm + 2 * halo
    r_refs = (r0_ref, r1_ref)

    def up_proj(r_ref, which):
        w_ref = (wg_ref, wu_ref)[which]
        r_ref[which] = jnp.dot(lhs_ref[...], w_ref[...], preferred_element_type=F32)

    def conv(win, cw_ref, cb_ref):
        n = win.shape[0]
        prev = pltpu.roll(win, 1, 0)[SUBLANES:n - SUBLANES, :]
        nxt = pltpu.roll(win, n - 1, 0)[SUBLANES:n - SUBLANES, :]
        return (prev * cw_ref[0:1, :] + win[SUBLANES:n - SUBLANES, :] * cw_ref[1:2, :]
                + nxt * cw_ref[2:3, :] + cb_ref[...])

    def down_proj(r_ref):
        rb = min(FFN_ACT_ROWS, tm)
        for b in range(tm // rb):
            lo = halo + b * rb - SUBLANES
            gate = conv(r_ref[0, lo:lo + rb + 2 * SUBLANES, :], cwg_ref, cbg_ref)
            up = conv(r_ref[1, lo:lo + rb + 2 * SUBLANES, :], cwu_ref, cbu_ref)
            act = (gate * jax.nn.sigmoid(gate) * up).astype(BF16)
            o_ref[b * rb:(b + 1) * rb, :] += jnp.dot(act, wd_ref[...], preferred_element_type=F32)

    @pl.when(f == 0)
    def _():
        lhs_ref[0:halo, :] = jnp.where(i > 0, hp_ref[...], jnp.zeros_like(hp_ref[...]))
        lhs_ref[halo:halo + tm, :] = hm_ref[...]
        lhs_ref[halo + tm:, :] = jnp.where(i < pl.num_programs(0) - 1, hn_ref[...],
                                           jnp.zeros_like(hn_ref[...]))
        o_ref[...] = jnp.zeros(o_ref.shape, F32)
        up_proj(r0_ref, 0)
        up_proj(r0_ref, 1)

    for parity in (0, 1):
        @pl.when(jnp.logical_and(jnp.logical_and(f > 0, f < nf), f % 2 == parity))
        def _(parity=parity):
            up_proj(r_refs[parity], 0)
            down_proj(r_refs[1 - parity])
            up_proj(r_refs[parity], 1)

    @pl.when(f == nf)
    def _():
        down_proj(r_refs[(nf - 1) % 2])


def _ffn(h, w_up, conv_w, conv_b, w_down):
    s, d = h.shape
    ff = w_down.shape[0]
    tm = min(FFN_TM, s)
    tf = min(FFN_TF, ff)
    assert ff % tf == 0 and s % tm == 0 and tm % BF16_ROWS == 0
    nf = ff // tf
    w_gu = w_up.astype(BF16).reshape(d, 2 * nf, tf).transpose(1, 0, 2)
    w_down = w_down.astype(BF16)
    hb = tm // BF16_ROWS
    n_hb = s // BF16_ROWS
    single = pl.Buffered(1)

    def cur(f):
        return jnp.minimum(f, nf - 1)

    def prv(f):
        return jnp.maximum(f - 1, 0)

    return pl.pallas_call(
        functools.partial(_ffn_kernel, nf=nf),
        grid=(s // tm, nf + 1),
        in_specs=[pl.BlockSpec((BF16_ROWS, d), lambda i, f: (jnp.maximum(i * hb - 1, 0), 0)),
                  pl.BlockSpec((tm, d), lambda i, f: (i, 0), pipeline_mode=single),
                  pl.BlockSpec((BF16_ROWS, d), lambda i, f: (jnp.minimum((i + 1) * hb, n_hb - 1), 0)),
                  pl.BlockSpec((None, d, tf), lambda i, f: (cur(f), 0, 0)),
                  pl.BlockSpec((None, d, tf), lambda i, f: (nf + cur(f), 0, 0)),
                  pl.BlockSpec((3, tf), lambda i, f: (0, prv(f))),
                  pl.BlockSpec((3, tf), lambda i, f: (0, nf + prv(f))),
                  pl.BlockSpec((1, tf), lambda i, f: (0, prv(f))),
                  pl.BlockSpec((1, tf), lambda i, f: (0, nf + prv(f))),
                  pl.BlockSpec((tf, d), lambda i, f: (prv(f), 0))],
        out_specs=pl.BlockSpec((tm, d), lambda i, f: (i, 0), pipeline_mode=single),
        out_shape=jax.ShapeDtypeStruct((s, d), F32),
        scratch_shapes=[pltpu.VMEM((tm + 2 * BF16_ROWS, d), BF16),
                        pltpu.VMEM((2, tm + 2 * BF16_ROWS, tf), F32),
                        pltpu.VMEM((2, tm + 2 * BF16_ROWS, tf), F32)],
        compiler_params=_params(2),
        name="ffn",
    )(h, h, h, w_gu, w_gu, conv_w, conv_w, conv_b, conv_b, w_down)


def _final_kernel(xm_ref, f_ref, g_ref, gate_ref, o_ref):
    o_ref[...] = xm_ref[...] + gate_ref[0:1, :] * _rms(f_ref[...], g_ref[...])


def _final(xm, f, g_post, mods):
    s, d = xm.shape
    tm = min(ROW_TILE, s)
    row = pl.BlockSpec((tm, d), lambda i: (i, 0))
    return pl.pallas_call(
        _final_kernel,
        grid=(s // tm,),
        in_specs=[row, row, pl.BlockSpec((1, d), lambda i: (0, 0)),
                  pl.BlockSpec((2, d), lambda i: (0, 5))],
        out_specs=row,
        out_shape=jax.ShapeDtypeStruct((s, d), F32),
        compiler_params=_params(1),
        name="final",
    )(xm, f, g_post.reshape(1, d), mods)


def _rope_tables(s):
    quarter = HEAD_DIM // 4
    pos = np.arange(s)
    freqs = ROPE_THETA ** (-np.arange(quarter, dtype=np.float64) / quarter)
    ang_r = (pos // GRID_W)[:, None] * freqs[None, :]
    ang_c = (pos % GRID_W)[:, None] * freqs[None, :]
    cos_t = np.concatenate([np.cos(ang_r), np.cos(ang_r), np.cos(ang_c), np.cos(ang_c)], axis=1)
    sin_t = np.concatenate([-np.sin(ang_r), np.sin(ang_r), -np.sin(ang_c), np.sin(ang_c)], axis=1)
    return jnp.asarray(cos_t, F32), jnp.asarray(sin_t, F32)


def kernel(x, c, ctx, c_ctx, w_ada, b_ada, g_pre_mix, g_post_mix, g_pre_ffn, g_post_ffn, w_in, ln_v_g,
           ln_v_b, w_s, b_s, g_q, g_k, g_out_a, g_out_b, w_out, w_up, conv_w, conv_b, w_down):
    bsz, s, d = x.shape
    assert bsz == 1 and w_in.shape[0] == 1, "single batch element, single layer"
    aw = ln_v_g.shape[-1]
    bw = g_out_b.shape[-1]
    in_cols = w_in.shape[-1]
    kvw = (in_cols - 2 * aw - bw) // 2
    n_heads, n_kv = bw // HEAD_DIM, kvw // HEAD_DIM
    q0, k0, v0 = 2 * aw, 2 * aw + bw, 2 * aw + bw + kvw
    ff = w_down.shape[1]

    x2 = x[0]
    ctx2 = ctx[0]
    w_in_b = w_in[0].astype(BF16)
    w_out_b = w_out[0].astype(BF16)

    mods = _adaln(jnp.concatenate([c, c_ctx[None, :]], axis=0), w_ada[0], b_ada[0])

    cos_t, sin_t = _rope_tables(s)
    h = _prenorm(x2, g_pre_mix[0], mods, 0, 0, 1)
    hc = _prenorm(ctx2, g_pre_mix[0], mods, 1, 0, 1)

    kinds = (["gelu"] * (2 * aw // LANES) + ["q"] * n_heads + ["k"] * n_kv + ["v"] * n_kv)
    z = _inproj(h, w_in_b, 0, kinds, cos_t, sin_t, g_q[0], g_k[0])
    zc = _inproj(hc, w_in_b, k0, ["kc"] * n_kv + ["v"] * n_kv, cos_t, sin_t, g_q[0], g_k[0])

    out_a = _gmlp(z, 2 * aw, ln_v_g[0], ln_v_b[0], w_s[0], b_s[0], g_out_a[0])
    out_b = _attention(z, q0, k0, v0, n_heads, n_kv, zc, g_out_b[0])
    o = _outproj(out_a, out_b, w_out_b)
    x_mix, h2 = _post(x2, o, g_post_mix[0], mods, g_pre_ffn[0])
    f = _ffn(h2, w_up[0], conv_w[0], conv_b[0].reshape(1, 2 * ff), w_down[0])
    out = _final(x_mix, f, g_post_ffn[0], mods)
    return out[None]
```

```python
import functools

import numpy as np
import jax
import jax.numpy as jnp
from jax import lax
from jax.experimental import pallas as pl
from jax.experimental.pallas import tpu as pltpu

GRID_W = 64
HEAD_DIM = 128
ROPE_THETA = 10000.0
EPS = 1e-6
N_MOD = 6
LOG2_E = 1.4426950408889634

LANES = 128
SUBLANES = 8
BF16_ROWS = 16
VMEM_LIMIT_BYTES = 56 * 1024 * 1024

ADALN_TN = 512
ROW_TILE = 256
INPROJ_TM = 512
INPROJ_TN = 1024
GMLP_CHUNKS_PER_STEP = 2
ATTN_TQ = 256
ATTN_TK = 1024
OUTPROJ_TM = 512
OUTPROJ_TN = 1024
FFN_TM = 1024
FFN_TF = 256
FFN_ACT_ROWS = 256

F32 = jnp.float32
BF16 = jnp.bfloat16


def _params(n_axes):
    return pltpu.CompilerParams(dimension_semantics=("arbitrary",) * n_axes,
                                vmem_limit_bytes=VMEM_LIMIT_BYTES)


def _rms(x, g):
    return x * lax.rsqrt(jnp.mean(x * x, axis=-1, keepdims=True) + EPS) * g


def _adaln_kernel(c_ref, w_ref, b_ref, o_ref, s_ref):
    k_dim, tn = w_ref.shape
    nj = tn // LANES

    @pl.when(pl.program_id(0) == 0)
    def _():
        c = c_ref[...]
        s_ref[...] = c * jax.nn.sigmoid(c)

    def body(r, accs):
        rows = pl.ds(pl.multiple_of(r * SUBLANES, SUBLANES), SUBLANES)
        s0 = s_ref[0, rows, :]
        s1 = s_ref[1, rows, :]
        new = []
        for j in range(nj):
            w = w_ref[rows, j * LANES:(j + 1) * LANES]
            new.append(accs[2 * j] + w * s0)
            new.append(accs[2 * j + 1] + w * s1)
        return tuple(new)

    zero = jnp.zeros((SUBLANES, LANES), F32)
    accs = lax.fori_loop(0, k_dim // SUBLANES, body, (zero,) * (2 * nj), unroll=4)
    for j in range(nj):
        cols = slice(j * LANES, (j + 1) * LANES)
        o_ref[0:1, cols] = jnp.sum(accs[2 * j], axis=0, keepdims=True) + b_ref[:, cols]
        o_ref[1:2, cols] = jnp.sum(accs[2 * j + 1], axis=0, keepdims=True) + b_ref[:, cols]


def _adaln(cc, w_ada, b_ada):
    k_dim, n = w_ada.shape
    tn = min(ADALN_TN, n)
    c_rep = jnp.broadcast_to(cc[:, :, None], (2, k_dim, LANES))
    return pl.pallas_call(
        _adaln_kernel,
        grid=(n // tn,),
        in_specs=[pl.BlockSpec((2, k_dim, LANES), lambda j: (0, 0, 0)),
                  pl.BlockSpec((k_dim, tn), lambda j: (0, j)),
                  pl.BlockSpec((1, tn), lambda j: (0, j))],
        out_specs=pl.BlockSpec((2, tn), lambda j: (0, j)),
        out_shape=jax.ShapeDtypeStruct((2, n), F32),
        scratch_shapes=[pltpu.VMEM((2, k_dim, LANES), F32)],
        compiler_params=_params(1),
        name="adaln",
    )(c_rep, w_ada, b_ada.reshape(1, n))


def _prenorm_kernel(x_ref, g_ref, sh_ref, sc_ref, o_ref, *, row):
    y = _rms(x_ref[...], g_ref[...])
    o_ref[...] = (y * (1.0 + sc_ref[row:row + 1, :]) + sh_ref[row:row + 1, :]).astype(o_ref.dtype)


def _prenorm(x, g, mods, row, shift_idx, scale_idx):
    s, d = x.shape
    tm = min(ROW_TILE, s)
    return pl.pallas_call(
        functools.partial(_prenorm_kernel, row=row),
        grid=(s // tm,),
        in_specs=[pl.BlockSpec((tm, d), lambda i: (i, 0)),
                  pl.BlockSpec((1, d), lambda i: (0, 0)),
                  pl.BlockSpec((2, d), lambda i: (0, shift_idx)),
                  pl.BlockSpec((2, d), lambda i: (0, scale_idx))],
        out_specs=pl.BlockSpec((tm, d), lambda i: (i, 0)),
        out_shape=jax.ShapeDtypeStruct((s, d), BF16),
        compiler_params=_params(1),
        name="prenorm",
    )(x, g.reshape(1, d), mods, mods)


def _swap_rope_halves(a):
    lane = lax.broadcasted_iota(jnp.int32, a.shape, 1)
    return jnp.where((lane & 32) != 0, pltpu.roll(a, 32, 1), pltpu.roll(a, LANES - 32, 1))


def _head_epilogue(a, kind, cos, sin, gq, gk):
    if kind == "gelu":
        return 0.5 * a * (1.0 + lax.erf(a * (2.0 ** -0.5)))
    if kind == "v":
        return a
    a = _rms(a, gq if kind == "q" else gk)
    if kind != "kc":
        a = a * cos + _swap_rope_halves(a) * sin
    if kind == "q":
        a = a * (HEAD_DIM ** -0.5 * LOG2_E)
    return a


def _side_cast_specs(arrays, n_steps, n_inner):
    in_specs, out_specs, out_shapes, counts = [], [], [], []
    for a in arrays:
        rows, cols = a.shape
        rb = next(r for r in range(BF16_ROWS, rows + 1, BF16_ROWS)
                  if rows % r == 0 and rows // r <= n_steps)
        nb = rows // rb

        def index(j, i, nb=nb):
            return (jnp.minimum(j * n_inner + i, nb - 1), 0)

        in_specs.append(pl.BlockSpec((rb, cols), index))
        out_specs.append(pl.BlockSpec((rb, cols), index))
        out_shapes.append(jax.ShapeDtypeStruct((rows, cols), BF16))
        counts.append(nb)
    return in_specs, out_specs, out_shapes, tuple(counts)


def _side_cast(src_refs, dst_refs, counts):
    step = pl.program_id(0) * pl.num_programs(1) + pl.program_id(1)
    for src, dst, nb in zip(src_refs, dst_refs, counts):
        @pl.when(step < nb)
        def _(src=src, dst=dst):
            dst[...] = src[...].astype(dst.dtype)


def _inproj_kernel(h_ref, w_ref, cos_ref, sin_ref, gq_ref, gk_ref, *refs, plans, side_counts):
    n_side = len(side_counts)
    o_ref = refs[n_side]
    _side_cast(refs[:n_side], refs[n_side + 1:], side_counts)
    j = pl.program_id(0)
    acc = jnp.dot(h_ref[...], w_ref[...], preferred_element_type=F32)
    for kinds, tiles in plans:
        cond = functools.reduce(jnp.logical_or, [j == t for t in tiles])

        @pl.when(cond)
        def _(kinds=kinds):
            for c, kind in enumerate(kinds):
                cols = slice(c * LANES, (c + 1) * LANES)
                o_ref[:, cols] = _head_epilogue(acc[:, cols], kind, cos_ref[...], sin_ref[...],
                                                gq_ref[...], gk_ref[...]).astype(o_ref.dtype)


def _inproj(h, w, col0, kinds_per_chunk, cos_t, sin_t, gq, gk, side=()):
    s, d = h.shape
    n = LANES * len(kinds_per_chunk)
    tm = min(INPROJ_TM, s)
    tn = min(INPROJ_TN, n)
    assert n % tn == 0 and col0 % tn == 0 and s % tm == 0
    per_tile = tn // LANES
    by_kinds = {}
    for t in range(n // tn):
        by_kinds.setdefault(tuple(kinds_per_chunk[t * per_tile:(t + 1) * per_tile]), []).append(t)
    plans = tuple((k, tuple(v)) for k, v in by_kinds.items())
    j0 = col0 // tn
    grid = (n // tn, s // tm)
    side_in, side_out, side_shapes, side_counts = _side_cast_specs(side, grid[0] * grid[1], grid[1])
    return pl.pallas_call(
        functools.partial(_inproj_kernel, plans=plans, side_counts=side_counts),
        grid=grid,
        in_specs=[pl.BlockSpec((tm, d), lambda j, i: (i, 0)),
                  pl.BlockSpec((d, tn), lambda j, i: (0, j + j0)),
                  pl.BlockSpec((tm, LANES), lambda j, i: (i, 0)),
                  pl.BlockSpec((tm, LANES), lambda j, i: (i, 0)),
                  pl.BlockSpec((1, LANES), lambda j, i: (0, 0)),
                  pl.BlockSpec((1, LANES), lambda j, i: (0, 0))] + side_in,
        out_specs=[pl.BlockSpec((tm, tn), lambda j, i: (i, j))] + side_out,
        out_shape=[jax.ShapeDtypeStruct((s, n), BF16)] + side_shapes,
        compiler_params=_params(2),
        name="inproj",
    )(h, w, cos_t, sin_t, gq.reshape(1, LANES), gk.reshape(1, LANES), *side)


def _gmlp_kernel(z_ref, lng_ref, lnb_ref, ws_ref, bs_ref, go_ref, o_ref, oa_ref, *, chunk, groups):
    aw = groups * LANES
    for cidx in range(z_ref.shape[0] // chunk):
        rows = slice(cidx * chunk, (cidx + 1) * chunk)
        v = z_ref[rows, aw:].astype(F32)
        mu = jnp.mean(v, axis=-1, keepdims=True)
        vc = v - mu
        var = jnp.mean(vc * vc, axis=-1, keepdims=True)
        vn = (vc * lax.rsqrt(var + EPS) * lng_ref[...] + lnb_ref[...]).astype(BF16)
        for g in range(groups):
            cols = slice(g * LANES, (g + 1) * LANES)
            mixed = jnp.dot(ws_ref[g], vn[:, cols], preferred_element_type=F32) + bs_ref[g]
            oa_ref[:, cols] = z_ref[rows, cols].astype(F32) * mixed
        o_ref[rows, :] = _rms(oa_ref[...], go_ref[...]).astype(o_ref.dtype)


def _gmlp(z, n_cols_a, ln_g, ln_b, w_s, b_s, g_out):
    s = z.shape[0]
    groups, chunk, _ = w_s.shape
    aw = n_cols_a // 2
    assert aw == groups * LANES
    rows = chunk * min(GMLP_CHUNKS_PER_STEP, s // chunk)
    bs_rep = jnp.broadcast_to(b_s[:, :, None], (groups, chunk, LANES))
    return pl.pallas_call(
        functools.partial(_gmlp_kernel, chunk=chunk, groups=groups),
        grid=(s // rows,),
        in_specs=[pl.BlockSpec((rows, 2 * aw), lambda i: (i, 0)),
                  pl.BlockSpec((1, aw), lambda i: (0, 0)),
                  pl.BlockSpec((1, aw), lambda i: (0, 0)),
                  pl.BlockSpec((groups, chunk, chunk), lambda i: (0, 0, 0)),
                  pl.BlockSpec((groups, chunk, LANES), lambda i: (0, 0, 0)),
                  pl.BlockSpec((1, aw), lambda i: (0, 0))],
        out_specs=pl.BlockSpec((rows, aw), lambda i: (i, 0)),
        out_shape=jax.ShapeDtypeStruct((s, aw), BF16),
        scratch_shapes=[pltpu.VMEM((chunk, aw), F32)],
        compiler_params=_params(1),
        name="gmlp",
    )(z, ln_g.reshape(1, aw), ln_b.reshape(1, aw), w_s.astype(BF16), bs_rep, g_out.reshape(1, aw))


def _attn_kernel(q_ref, kc_ref, vc_ref, k_ref, v_ref, go_ref, o_ref, ob_ref, *, group, tk):
    kh = pl.program_id(1)
    n_kh = pl.num_programs(1)

    chunks = [(kc_ref, 0, kc_ref.shape[0])]
    chunks += [(k_ref, c * tk, tk) for c in range(k_ref.shape[0] // tk)]
    v_refs = {id(kc_ref): vc_ref, id(k_ref): v_ref}
    tiles = [(ci, g) for ci in range(len(chunks)) for g in range(group)]

    def scores(tile):
        ci, g = tile
        ref, lo, n = chunks[ci]
        q = q_ref[:, g * HEAD_DIM:(g + 1) * HEAD_DIM]
        return lax.dot_general(q, ref[lo:lo + n, :], (((1,), (1,)), ((), ())),
                               preferred_element_type=F32)

    stats = [None] * group
    accs = [None] * group

    def softmax(tile, s):
        g = tile[1]
        m_blk = jnp.max(s, axis=-1, keepdims=True)
        if stats[g] is None:
            p = jnp.exp2(s - m_blk)
            stats[g] = (m_blk, jnp.sum(p, axis=-1, keepdims=True))
            return tile, p.astype(BF16), None
        m_old, l_old = stats[g]
        m_new = jnp.maximum(m_old, m_blk)
        p = jnp.exp2(s - m_new)
        alpha = jnp.exp2(m_old - m_new)
        stats[g] = (m_new, alpha * l_old + jnp.sum(p, axis=-1, keepdims=True))
        return tile, p.astype(BF16), alpha

    def values(pending):
        (ci, g), p, alpha = pending
        ref, lo, n = chunks[ci]
        pv = jnp.dot(p, v_refs[id(ref)][lo:lo + n, :], preferred_element_type=F32)
        accs[g] = pv if alpha is None else alpha * accs[g] + pv

    s_next = scores(tiles[0])
    pending = None
    for t, tile in enumerate(tiles):
        s_cur = s_next
        if t + 1 < len(tiles):
            s_next = scores(tiles[t + 1])
        done = softmax(tile, s_cur)
        if pending is not None:
            values(pending)
        pending = done
    values(pending)

    for g in range(group):
        ob_ref[kh * group + g] = accs[g] / stats[g][1]

    @pl.when(kh == n_kh - 1)
    def _():
        n_heads = ob_ref.shape[0]
        ssq = jnp.zeros((q_ref.shape[0], 1), F32)
        for h in range(n_heads):
            o = ob_ref[h]
            ssq = ssq + jnp.sum(o * o, axis=-1, keepdims=True)
        inv = lax.rsqrt(ssq / (n_heads * HEAD_DIM) + EPS)
        for h in range(n_heads):
            cols = slice(h * HEAD_DIM, (h + 1) * HEAD_DIM)
            o_ref[:, cols] = (ob_ref[h] * inv * go_ref[:, cols]).astype(o_ref.dtype)


def _attention(z, q_col0, k_col0, v_col0, n_heads, n_kv, zc, g_out):
    s = z.shape[0]
    c = zc.shape[0]
    group = n_heads // n_kv
    gw = group * HEAD_DIM
    bw = n_heads * HEAD_DIM
    tq = min(ATTN_TQ, s)
    tk = min(ATTN_TK, s)
    assert q_col0 % gw == 0 and k_col0 % HEAD_DIM == 0 and v_col0 % HEAD_DIM == 0
    qb, kb, vb = q_col0 // gw, k_col0 // HEAD_DIM, v_col0 // HEAD_DIM
    return pl.pallas_call(
        functools.partial(_attn_kernel, group=group, tk=tk),
        grid=(s // tq, n_kv),
        in_specs=[pl.BlockSpec((tq, gw), lambda i, kh: (i, qb + kh)),
                  pl.BlockSpec((c, HEAD_DIM), lambda i, kh: (0, kh)),
                  pl.BlockSpec((c, HEAD_DIM), lambda i, kh: (0, n_kv + kh)),
                  pl.BlockSpec((s, HEAD_DIM), lambda i, kh: (0, kb + kh)),
                  pl.BlockSpec((s, HEAD_DIM), lambda i, kh: (0, vb + kh)),
                  pl.BlockSpec((1, bw), lambda i, kh: (0, 0))],
        out_specs=pl.BlockSpec((tq, bw), lambda i, kh: (i, 0)),
        out_shape=jax.ShapeDtypeStruct((s, bw), BF16),
        scratch_shapes=[pltpu.VMEM((n_heads, tq, HEAD_DIM), F32)],
        compiler_params=_params(2),
        name="attn",
    )(z, zc, zc, z, z, g_out.reshape(1, bw))


def _outproj_kernel(a_ref, b_ref, wa_ref, wb_ref, *refs, side_counts):
    n_side = len(side_counts)
    o_ref = refs[n_side]
    _side_cast(refs[:n_side], refs[n_side + 1:], side_counts)
    o_ref[...] = (jnp.dot(a_ref[...], wa_ref[...], preferred_element_type=F32)
                  + jnp.dot(b_ref[...], wb_ref[...], preferred_element_type=F32))


def _outproj(a, b, w, side=()):
    s, aw = a.shape
    bw = b.shape[1]
    n = w.shape[1]
    tm = min(OUTPROJ_TM, s)
    tn = min(OUTPROJ_TN, n)
    assert aw == bw
    grid = (n // tn, s // tm)
    side_in, side_out, side_shapes, side_counts = _side_cast_specs(side, grid[0] * grid[1], grid[1])
    return pl.pallas_call(
        functools.partial(_outproj_kernel, side_counts=side_counts),
        grid=grid,
        in_specs=[pl.BlockSpec((tm, aw), lambda j, i: (i, 0)),
                  pl.BlockSpec((tm, bw), lambda j, i: (i, 0)),
                  pl.BlockSpec((aw, tn), lambda j, i: (0, j)),
                  pl.BlockSpec((bw, tn), lambda j, i: (1, j))] + side_in,
        out_specs=[pl.BlockSpec((tm, tn), lambda j, i: (i, j))] + side_out,
        out_shape=[jax.ShapeDtypeStruct((s, n), F32)] + side_shapes,
        compiler_params=_params(2),
        name="outproj",
    )(a, b, w, w, *side)


def _post_kernel(x_ref, o_ref, gpost_ref, gate_ref, gpre_ref, sh_ref, sc_ref, xm_ref, h_ref):
    xm = x_ref[...] + gate_ref[0:1, :] * _rms(o_ref[...], gpost_ref[...])
    xm_ref[...] = xm
    h = _rms(xm, gpre_ref[...])
    h_ref[...] = (h * (1.0 + sc_ref[0:1, :]) + sh_ref[0:1, :]).astype(h_ref.dtype)


def _post(x, o, g_post, mods, g_pre):
    s, d = x.shape
    tm = min(ROW_TILE, s)
    row = pl.BlockSpec((tm, d), lambda i: (i, 0))
    vec = pl.BlockSpec((1, d), lambda i: (0, 0))

    def mod(idx):
        return pl.BlockSpec((2, d), lambda i: (0, idx))

    return pl.pallas_call(
        _post_kernel,
        grid=(s // tm,),
        in_specs=[row, row, vec, mod(2), vec, mod(3), mod(4)],
        out_specs=[row, row],
        out_shape=[jax.ShapeDtypeStruct((s, d), F32), jax.ShapeDtypeStruct((s, d), BF16)],
        compiler_params=_params(1),
        name="post",
    )(x, o, g_post.reshape(1, d), mods, g_pre.reshape(1, d), mods, mods)


def _ffn_kernel(hp_ref, hm_ref, hn_ref, wg_ref, wu_ref, cwg_ref, cwu_ref, cbg_ref, cbu_ref, wd_ref,
                o_ref, lhs_ref, r0_ref, r1_ref, act_ref, *, nf):
    i = pl.program_id(0)
    f = pl.program_id(1)
    tm = hm_ref.shape[0]
    halo = hp_ref.shape[0]
    rows = tm + 2 * halo
    r_refs = (r0_ref, r1_ref)

    def up_proj(r_ref, which):
        w_ref = (wg_ref, wu_ref)[which]
        r_ref[which] = jnp.dot(lhs_ref[...], w_ref[...], preferred_element_type=F32)

    def conv(win, cw_ref, cb_ref):
        n = win.shape[0]
        prev = pltpu.roll(win, 1, 0)[SUBLANES:n - SUBLANES, :]
        nxt = pltpu.roll(win, n - 1, 0)[SUBLANES:n - SUBLANES, :]
        return (prev * cw_ref[0:1, :] + win[SUBLANES:n - SUBLANES, :] * cw_ref[1:2, :]
                + nxt * cw_ref[2:3, :] + cb_ref[...])

    def activation(r_ref):
        rb = min(FFN_ACT_ROWS, tm)
        for b in range(tm // rb):
            lo = halo + b * rb - SUBLANES
            gate = conv(r_ref[0, lo:lo + rb + 2 * SUBLANES, :], cwg_ref, cbg_ref)
            up = conv(r_ref[1, lo:lo + rb + 2 * SUBLANES, :], cwu_ref, cbu_ref)
            act_ref[b * rb:(b + 1) * rb, :] = (gate * jax.nn.sigmoid(gate) * up).astype(BF16)

    def down_proj():
        o_ref[...] += jnp.dot(act_ref[...], wd_ref[...], preferred_element_type=F32)

    @pl.when(f == 0)
    def _():
        lhs_ref[0:halo, :] = jnp.where(i > 0, hp_ref[...], jnp.zeros_like(hp_ref[...]))
        lhs_ref[halo:halo + tm, :] = hm_ref[...]
        lhs_ref[halo + tm:, :] = jnp.where(i < pl.num_programs(0) - 1, hn_ref[...],
                                           jnp.zeros_like(hn_ref[...]))
        o_ref[...] = jnp.zeros(o_ref.shape, F32)
        up_proj(r0_ref, 0)
        up_proj(r0_ref, 1)

    for parity in (0, 1):
        @pl.when(jnp.logical_and(jnp.logical_and(f > 0, f < nf), f % 2 == parity))
        def _(parity=parity):
            activation(r_refs[1 - parity])
            up_proj(r_refs[parity], 0)
            down_proj()
            up_proj(r_refs[parity], 1)

    @pl.when(f == nf)
    def _():
        activation(r_refs[(nf - 1) % 2])
        down_proj()


def _ffn(h, w_gu, conv_w, conv_b, w_down):
    s, d = h.shape
    ff = w_down.shape[0]
    tm = min(FFN_TM, s)
    tf = min(FFN_TF, ff)
    assert ff % tf == 0 and s % tm == 0 and tm % BF16_ROWS == 0
    nf = ff // tf
    hb = tm // BF16_ROWS
    n_hb = s // BF16_ROWS
    single = pl.Buffered(1)

    def cur(f):
        return jnp.minimum(f, nf - 1)

    def prv(f):
        return jnp.maximum(f - 1, 0)

    return pl.pallas_call(
        functools.partial(_ffn_kernel, nf=nf),
        grid=(s // tm, nf + 1),
        in_specs=[pl.BlockSpec((BF16_ROWS, d), lambda i, f: (jnp.maximum(i * hb - 1, 0), 0)),
                  pl.BlockSpec((tm, d), lambda i, f: (i, 0), pipeline_mode=single),
                  pl.BlockSpec((BF16_ROWS, d), lambda i, f: (jnp.minimum((i + 1) * hb, n_hb - 1), 0)),
                  pl.BlockSpec((d, tf), lambda i, f: (0, cur(f))),
                  pl.BlockSpec((d, tf), lambda i, f: (0, nf + cur(f))),
                  pl.BlockSpec((3, tf), lambda i, f: (0, prv(f))),
                  pl.BlockSpec((3, tf), lambda i, f: (0, nf + prv(f))),
                  pl.BlockSpec((1, tf), lambda i, f: (0, prv(f))),
                  pl.BlockSpec((1, tf), lambda i, f: (0, nf + prv(f))),
                  pl.BlockSpec((tf, d), lambda i, f: (prv(f), 0))],
        out_specs=pl.BlockSpec((tm, d), lambda i, f: (i, 0), pipeline_mode=single),
        out_shape=jax.ShapeDtypeStruct((s, d), F32),
        scratch_shapes=[pltpu.VMEM((tm + 2 * BF16_ROWS, d), BF16),
                        pltpu.VMEM((2, tm + 2 * BF16_ROWS, tf), F32),
                        pltpu.VMEM((2, tm + 2 * BF16_ROWS, tf), F32),
                        pltpu.VMEM((tm, tf), BF16)],
        compiler_params=_params(2),
        name="ffn",
    )(h, h, h, w_gu, w_gu, conv_w, conv_w, conv_b, conv_b, w_down)


def _final_kernel(xm_ref, f_ref, g_ref, gate_ref, o_ref):
    o_ref[...] = xm_ref[...] + gate_ref[0:1, :] * _rms(f_ref[...], g_ref[...])


def _final(xm, f, g_post, mods):
    s, d = xm.shape
    tm = min(ROW_TILE, s)
    row = pl.BlockSpec((tm, d), lambda i: (i, 0))
    return pl.pallas_call(
        _final_kernel,
        grid=(s // tm,),
        in_specs=[row, row, pl.BlockSpec((1, d), lambda i: (0, 0)),
                  pl.BlockSpec((2, d), lambda i: (0, 5))],
        out_specs=row,
        out_shape=jax.ShapeDtypeStruct((s, d), F32),
        compiler_params=_params(1),
        name="final",
    )(xm, f, g_post.reshape(1, d), mods)


def _rope_tables(s):
    quarter = HEAD_DIM // 4
    pos = np.arange(s)
    freqs = ROPE_THETA ** (-np.arange(quarter, dtype=np.float64) / quarter)
    ang_r = (pos // GRID_W)[:, None] * freqs[None, :]
    ang_c = (pos % GRID_W)[:, None] * freqs[None, :]
    cos_t = np.concatenate([np.cos(ang_r), np.cos(ang_r), np.cos(ang_c), np.cos(ang_c)], axis=1)
    sin_t = np.concatenate([-np.sin(ang_r), np.sin(ang_r), -np.sin(ang_c), np.sin(ang_c)], axis=1)
    return jnp.asarray(cos_t, F32), jnp.asarray(sin_t, F32)


def kernel(x, c, ctx, c_ctx, w_ada, b_ada, g_pre_mix, g_post_mix, g_pre_ffn, g_post_ffn, w_in, ln_v_g,
           ln_v_b, w_s, b_s, g_q, g_k, g_out_a, g_out_b, w_out, w_up, conv_w, conv_b, w_down):
    bsz, s, d = x.shape
    assert bsz == 1 and w_in.shape[0] == 1, "single batch element, single layer"
    aw = ln_v_g.shape[-1]
    bw = g_out_b.shape[-1]
    in_cols = w_in.shape[-1]
    kvw = (in_cols - 2 * aw - bw) // 2
    n_heads, n_kv = bw // HEAD_DIM, kvw // HEAD_DIM
    q0, k0, v0 = 2 * aw, 2 * aw + bw, 2 * aw + bw + kvw
    ff = w_down.shape[1]

    x2 = x[0]
    ctx2 = ctx[0]
    w_in_b = w_in[0].astype(BF16)

    mods = _adaln(jnp.concatenate([c, c_ctx[None, :]], axis=0), w_ada[0], b_ada[0])

    cos_t, sin_t = _rope_tables(s)
    h = _prenorm(x2, g_pre_mix[0], mods, 0, 0, 1)
    hc = _prenorm(ctx2, g_pre_mix[0], mods, 1, 0, 1)

    kinds = (["gelu"] * (2 * aw // LANES) + ["q"] * n_heads + ["k"] * n_kv + ["v"] * n_kv)
    z, w_gu_b, w_out_b = _inproj(h, w_in_b, 0, kinds, cos_t, sin_t, g_q[0], g_k[0],
                                 side=(w_up[0], w_out[0]))
    zc, = _inproj(hc, w_in_b, k0, ["kc"] * n_kv + ["v"] * n_kv, cos_t, sin_t, g_q[0], g_k[0])

    out_a = _gmlp(z, 2 * aw, ln_v_g[0], ln_v_b[0], w_s[0], b_s[0], g_out_a[0])
    out_b = _attention(z, q0, k0, v0, n_heads, n_kv, zc, g_out_b[0])
    o, w_down_b = _outproj(out_a, out_b, w_out_b, side=(w_down[0],))
    x_mix, h2 = _post(x2, o, g_post_mix[0], mods, g_pre_ffn[0])
    f = _ffn(h2, w_gu_b, conv_w[0], conv_b[0].reshape(1, 2 * ff), w_down_b)
    out = _final(x_mix, f, g_post_ffn[0], mods)
    return out[None]
```

```python
import functools

import numpy as np
import jax
import jax.numpy as jnp
from jax import lax
from jax.experimental import pallas as pl
from jax.experimental.pallas import tpu as pltpu

GRID_W = 64
HEAD_DIM = 128
ROPE_THETA = 10000.0
EPS = 1e-6
N_MOD = 6
LOG2_E = 1.4426950408889634

LANES = 128
SUBLANES = 8
BF16_ROWS = 16
VMEM_LIMIT_BYTES = 56 * 1024 * 1024

ADALN_TN = 1024
ROW_TILE = 256
INPROJ_TM = 512
INPROJ_TN = 1024
GMLP_CHUNKS_PER_STEP = 2
ATTN_TQ = 256
ATTN_TK = 1024
OUTPROJ_TM = 512
OUTPROJ_TN = 1024
FFN_TM = 1024
FFN_TF = 256
FFN_ACT_ROWS = 256

F32 = jnp.float32
BF16 = jnp.bfloat16


def _params(n_axes):
    return pltpu.CompilerParams(dimension_semantics=("arbitrary",) * n_axes,
                                vmem_limit_bytes=VMEM_LIMIT_BYTES)


def _rms(x, g):
    return x * lax.rsqrt(jnp.mean(x * x, axis=-1, keepdims=True) + EPS) * g


def _adaln_kernel(c_ref, w_ref, b_ref, o_ref, s_ref):
    k_dim, tn = w_ref.shape
    nj = tn // LANES

    @pl.when(pl.program_id(0) == 0)
    def _():
        c = c_ref[...]
        s_ref[...] = c * jax.nn.sigmoid(c)

    def body(r, accs):
        rows = pl.ds(pl.multiple_of(r * SUBLANES, SUBLANES), SUBLANES)
        s0 = s_ref[0, rows, :]
        s1 = s_ref[1, rows, :]
        new = []
        for j in range(nj):
            w = w_ref[rows, j * LANES:(j + 1) * LANES]
            new.append(accs[2 * j] + w * s0)
            new.append(accs[2 * j + 1] + w * s1)
        return tuple(new)

    zero = jnp.zeros((SUBLANES, LANES), F32)
    accs = lax.fori_loop(0, k_dim // SUBLANES, body, (zero,) * (2 * nj), unroll=4)
    for j in range(nj):
        cols = slice(j * LANES, (j + 1) * LANES)
        o_ref[0:1, cols] = jnp.sum(accs[2 * j], axis=0, keepdims=True) + b_ref[:, cols]
        o_ref[1:2, cols] = jnp.sum(accs[2 * j + 1], axis=0, keepdims=True) + b_ref[:, cols]


def _adaln(cc, w_ada, b_ada):
    k_dim, n = w_ada.shape
    tn = min(ADALN_TN, n)
    c_rep = jnp.broadcast_to(cc[:, :, None], (2, k_dim, LANES))
    return pl.pallas_call(
        _adaln_kernel,
        grid=(n // tn,),
        in_specs=[pl.BlockSpec((2, k_dim, LANES), lambda j: (0, 0, 0)),
                  pl.BlockSpec((k_dim, tn), lambda j: (0, j)),
                  pl.BlockSpec((1, tn), lambda j: (0, j))],
        out_specs=pl.BlockSpec((2, tn), lambda j: (0, j)),
        out_shape=jax.ShapeDtypeStruct((2, n), F32),
        scratch_shapes=[pltpu.VMEM((2, k_dim, LANES), F32)],
        compiler_params=_params(1),
        name="adaln",
    )(c_rep, w_ada, b_ada.reshape(1, n))


def _prenorm_kernel(x_ref, g_ref, sh_ref, sc_ref, o_ref, *, row):
    y = _rms(x_ref[...], g_ref[...])
    o_ref[...] = (y * (1.0 + sc_ref[row:row + 1, :]) + sh_ref[row:row + 1, :]).astype(o_ref.dtype)


def _prenorm(x, g, mods, row, shift_idx, scale_idx):
    s, d = x.shape
    tm = min(ROW_TILE, s)
    return pl.pallas_call(
        functools.partial(_prenorm_kernel, row=row),
        grid=(s // tm,),
        in_specs=[pl.BlockSpec((tm, d), lambda i: (i, 0)),
                  pl.BlockSpec((1, d), lambda i: (0, 0)),
                  pl.BlockSpec((2, d), lambda i: (0, shift_idx)),
                  pl.BlockSpec((2, d), lambda i: (0, scale_idx))],
        out_specs=pl.BlockSpec((tm, d), lambda i: (i, 0)),
        out_shape=jax.ShapeDtypeStruct((s, d), BF16),
        compiler_params=_params(1),
        name="prenorm",
    )(x, g.reshape(1, d), mods, mods)


def _swap_rope_halves(a):
    lane = lax.broadcasted_iota(jnp.int32, a.shape, 1)
    return jnp.where((lane & 32) != 0, pltpu.roll(a, 32, 1), pltpu.roll(a, LANES - 32, 1))


def _head_epilogue(a, kind, cos, sin, gq, gk):
    if kind == "gelu":
        return 0.5 * a * (1.0 + lax.erf(a * (2.0 ** -0.5)))
    if kind == "v":
        return a
    a = _rms(a, gq if kind == "q" else gk)
    if kind != "kc":
        a = a * cos + _swap_rope_halves(a) * sin
    if kind == "q":
        a = a * (HEAD_DIM ** -0.5 * LOG2_E)
    return a


def _side_cast_specs(arrays, n_steps, n_inner):
    in_specs, out_specs, out_shapes, counts = [], [], [], []
    for a in arrays:
        rows, cols = a.shape
        rb = next(r for r in range(BF16_ROWS, rows + 1, BF16_ROWS)
                  if rows % r == 0 and rows // r <= n_steps)
        nb = rows // rb

        def index(j, i, nb=nb):
            return (jnp.minimum(j * n_inner + i, nb - 1), 0)

        in_specs.append(pl.BlockSpec((rb, cols), index))
        out_specs.append(pl.BlockSpec((rb, cols), index))
        out_shapes.append(jax.ShapeDtypeStruct((rows, cols), BF16))
        counts.append(nb)
    return in_specs, out_specs, out_shapes, tuple(counts)


def _side_cast(src_refs, dst_refs, counts):
    step = pl.program_id(0) * pl.num_programs(1) + pl.program_id(1)
    for src, dst, nb in zip(src_refs, dst_refs, counts):
        @pl.when(step < nb)
        def _(src=src, dst=dst):
            dst[...] = src[...].astype(dst.dtype)


def _inproj_kernel(h_ref, w_ref, cos_ref, sin_ref, gq_ref, gk_ref, *refs, plans, side_counts):
    n_side = len(side_counts)
    o_ref = refs[n_side]
    _side_cast(refs[:n_side], refs[n_side + 1:], side_counts)
    j = pl.program_id(0)
    for kinds, tiles in plans:
        cond = functools.reduce(jnp.logical_or, [j == t for t in tiles])

        @pl.when(cond)
        def _(kinds=kinds):
            acc = jnp.dot(h_ref[...], w_ref[...], preferred_element_type=F32)
            for c, kind in enumerate(kinds):
                cols = slice(c * LANES, (c + 1) * LANES)
                o_ref[:, cols] = _head_epilogue(acc[:, cols], kind, cos_ref[...], sin_ref[...],
                                                gq_ref[...], gk_ref[...]).astype(o_ref.dtype)


def _inproj(h, w, col0, kinds_per_chunk, cos_t, sin_t, gq, gk, side=()):
    s, d = h.shape
    n = LANES * len(kinds_per_chunk)
    tm = min(INPROJ_TM, s)
    tn = min(INPROJ_TN, n)
    assert n % tn == 0 and col0 % tn == 0 and s % tm == 0
    per_tile = tn // LANES
    by_kinds = {}
    for t in range(n // tn):
        by_kinds.setdefault(tuple(kinds_per_chunk[t * per_tile:(t + 1) * per_tile]), []).append(t)
    plans = tuple((k, tuple(v)) for k, v in by_kinds.items())
    j0 = col0 // tn
    grid = (n // tn, s // tm)
    side_in, side_out, side_shapes, side_counts = _side_cast_specs(side, grid[0] * grid[1], grid[1])
    return pl.pallas_call(
        functools.partial(_inproj_kernel, plans=plans, side_counts=side_counts),
        grid=grid,
        in_specs=[pl.BlockSpec((tm, d), lambda j, i: (i, 0)),
                  pl.BlockSpec((d, tn), lambda j, i: (0, j + j0)),
                  pl.BlockSpec((tm, LANES), lambda j, i: (i, 0)),
                  pl.BlockSpec((tm, LANES), lambda j, i: (i, 0)),
                  pl.BlockSpec((1, LANES), lambda j, i: (0, 0)),
                  pl.BlockSpec((1, LANES), lambda j, i: (0, 0))] + side_in,
        out_specs=[pl.BlockSpec((tm, tn), lambda j, i: (i, j))] + side_out,
        out_shape=[jax.ShapeDtypeStruct((s, n), BF16)] + side_shapes,
        compiler_params=_params(2),
        name="inproj",
    )(h, w, cos_t, sin_t, gq.reshape(1, LANES), gk.reshape(1, LANES), *side)


def _gmlp_kernel(z_ref, lng_ref, lnb_ref, ws_ref, bs_ref, go_ref, o_ref, oa_ref, *, chunk, groups):
    aw = groups * LANES
    for cidx in range(z_ref.shape[0] // chunk):
        rows = slice(cidx * chunk, (cidx + 1) * chunk)
        v = z_ref[rows, aw:].astype(F32)
        mu = jnp.mean(v, axis=-1, keepdims=True)
        vc = v - mu
        var = jnp.mean(vc * vc, axis=-1, keepdims=True)
        vn = (vc * lax.rsqrt(var + EPS) * lng_ref[...] + lnb_ref[...]).astype(BF16)
        for g in range(groups):
            cols = slice(g * LANES, (g + 1) * LANES)
            mixed = jnp.dot(ws_ref[g], vn[:, cols], preferred_element_type=F32) + bs_ref[g]
            oa_ref[:, cols] = z_ref[rows, cols].astype(F32) * mixed
        o_ref[rows, :] = _rms(oa_ref[...], go_ref[...]).astype(o_ref.dtype)


def _gmlp(z, n_cols_a, ln_g, ln_b, w_s, b_s, g_out):
    s = z.shape[0]
    groups, chunk, _ = w_s.shape
    aw = n_cols_a // 2
    assert aw == groups * LANES
    rows = chunk * min(GMLP_CHUNKS_PER_STEP, s // chunk)
    bs_rep = jnp.broadcast_to(b_s[:, :, None], (groups, chunk, LANES))
    return pl.pallas_call(
        functools.partial(_gmlp_kernel, chunk=chunk, groups=groups),
        grid=(s // rows,),
        in_specs=[pl.BlockSpec((rows, 2 * aw), lambda i: (i, 0)),
                  pl.BlockSpec((1, aw), lambda i: (0, 0)),
                  pl.BlockSpec((1, aw), lambda i: (0, 0)),
                  pl.BlockSpec((groups, chunk, chunk), lambda i: (0, 0, 0)),
                  pl.BlockSpec((groups, chunk, LANES), lambda i: (0, 0, 0)),
                  pl.BlockSpec((1, aw), lambda i: (0, 0))],
        out_specs=pl.BlockSpec((rows, aw), lambda i: (i, 0)),
        out_shape=jax.ShapeDtypeStruct((s, aw), BF16),
        scratch_shapes=[pltpu.VMEM((chunk, aw), F32)],
        compiler_params=_params(1),
        name="gmlp",
    )(z, ln_g.reshape(1, aw), ln_b.reshape(1, aw), w_s.astype(BF16), bs_rep, g_out.reshape(1, aw))


def _attn_kernel(q_ref, kc_ref, vc_ref, k_ref, v_ref, go_ref, o_ref, ob_ref, *, group, tk):
    kh = pl.program_id(1)
    n_kh = pl.num_programs(1)

    chunks = [(kc_ref, 0, kc_ref.shape[0])]
    chunks += [(k_ref, c * tk, tk) for c in range(k_ref.shape[0] // tk)]
    v_refs = {id(kc_ref): vc_ref, id(k_ref): v_ref}
    tiles = [(ci, g) for ci in range(len(chunks)) for g in range(group)]

    def scores(tile):
        ci, g = tile
        ref, lo, n = chunks[ci]
        q = q_ref[:, g * HEAD_DIM:(g + 1) * HEAD_DIM]
        return lax.dot_general(q, ref[lo:lo + n, :], (((1,), (1,)), ((), ())),
                               preferred_element_type=F32)

    stats = [None] * group
    accs = [None] * group

    def softmax(tile, s):
        g = tile[1]
        m_blk = jnp.max(s, axis=-1, keepdims=True)
        if stats[g] is None:
            p = jnp.exp2(s - m_blk)
            stats[g] = (m_blk, jnp.sum(p, axis=-1, keepdims=True))
            return tile, p.astype(BF16), None
        m_old, l_old = stats[g]
        m_new = jnp.maximum(m_old, m_blk)
        p = jnp.exp2(s - m_new)
        alpha = jnp.exp2(m_old - m_new)
        stats[g] = (m_new, alpha * l_old + jnp.sum(p, axis=-1, keepdims=True))
        return tile, p.astype(BF16), alpha

    def values(pending):
        (ci, g), p, alpha = pending
        ref, lo, n = chunks[ci]
        pv = jnp.dot(p, v_refs[id(ref)][lo:lo + n, :], preferred_element_type=F32)
        accs[g] = pv if alpha is None else alpha * accs[g] + pv

    s_next = scores(tiles[0])
    pending = None
    for t, tile in enumerate(tiles):
        s_cur = s_next
        if t + 1 < len(tiles):
            s_next = scores(tiles[t + 1])
        done = softmax(tile, s_cur)
        if pending is not None:
            values(pending)
        pending = done
    values(pending)

    for g in range(group):
        ob_ref[kh * group + g] = accs[g] / stats[g][1]

    @pl.when(kh == n_kh - 1)
    def _():
        n_heads = ob_ref.shape[0]
        ssq = jnp.zeros((q_ref.shape[0], 1), F32)
        for h in range(n_heads):
            o = ob_ref[h]
            ssq = ssq + jnp.sum(o * o, axis=-1, keepdims=True)
        inv = lax.rsqrt(ssq / (n_heads * HEAD_DIM) + EPS)
        for h in range(n_heads):
            cols = slice(h * HEAD_DIM, (h + 1) * HEAD_DIM)
            o_ref[:, cols] = (ob_ref[h] * inv * go_ref[:, cols]).astype(o_ref.dtype)


def _attention(z, q_col0, k_col0, v_col0, n_heads, n_kv, zc, g_out):
    s = z.shape[0]
    c = zc.shape[0]
    group = n_heads // n_kv
    gw = group * HEAD_DIM
    bw = n_heads * HEAD_DIM
    tq = min(ATTN_TQ, s)
    tk = min(ATTN_TK, s)
    assert q_col0 % gw == 0 and k_col0 % HEAD_DIM == 0 and v_col0 % HEAD_DIM == 0
    qb, kb, vb = q_col0 // gw, k_col0 // HEAD_DIM, v_col0 // HEAD_DIM
    return pl.pallas_call(
        functools.partial(_attn_kernel, group=group, tk=tk),
        grid=(s // tq, n_kv),
        in_specs=[pl.BlockSpec((tq, gw), lambda i, kh: (i, qb + kh)),
                  pl.BlockSpec((c, HEAD_DIM), lambda i, kh: (0, kh)),
                  pl.BlockSpec((c, HEAD_DIM), lambda i, kh: (0, n_kv + kh)),
                  pl.BlockSpec((s, HEAD_DIM), lambda i, kh: (0, kb + kh)),
                  pl.BlockSpec((s, HEAD_DIM), lambda i, kh: (0, vb + kh)),
                  pl.BlockSpec((1, bw), lambda i, kh: (0, 0))],
        out_specs=pl.BlockSpec((tq, bw), lambda i, kh: (i, 0)),
        out_shape=jax.ShapeDtypeStruct((s, bw), BF16),
        scratch_shapes=[pltpu.VMEM((n_heads, tq, HEAD_DIM), F32)],
        compiler_params=_params(2),
        name="attn",
    )(z, zc, zc, z, z, g_out.reshape(1, bw))


def _outproj_kernel(a_ref, b_ref, wa_ref, wb_ref, *refs, side_counts):
    n_side = len(side_counts)
    o_ref = refs[n_side]
    _side_cast(refs[:n_side], refs[n_side + 1:], side_counts)
    o_ref[...] = (jnp.dot(a_ref[...], wa_ref[...], preferred_element_type=F32)
                  + jnp.dot(b_ref[...], wb_ref[...], preferred_element_type=F32))


def _outproj(a, b, w, side=()):
    s, aw = a.shape
    bw = b.shape[1]
    n = w.shape[1]
    tm = min(OUTPROJ_TM, s)
    tn = min(OUTPROJ_TN, n)
    assert aw == bw
    grid = (n // tn, s // tm)
    side_in, side_out, side_shapes, side_counts = _side_cast_specs(side, grid[0] * grid[1], grid[1])
    return pl.pallas_call(
        functools.partial(_outproj_kernel, side_counts=side_counts),
        grid=grid,
        in_specs=[pl.BlockSpec((tm, aw), lambda j, i: (i, 0)),
                  pl.BlockSpec((tm, bw), lambda j, i: (i, 0)),
                  pl.BlockSpec((aw, tn), lambda j, i: (0, j)),
                  pl.BlockSpec((bw, tn), lambda j, i: (1, j))] + side_in,
        out_specs=[pl.BlockSpec((tm, tn), lambda j, i: (i, j))] + side_out,
        out_shape=[jax.ShapeDtypeStruct((s, n), F32)] + side_shapes,
        compiler_params=_params(2),
        name="outproj",
    )(a, b, w, w, *side)


def _post_kernel(x_ref, o_ref, gpost_ref, gate_ref, gpre_ref, sh_ref, sc_ref, xm_ref, h_ref):
    xm = x_ref[...] + gate_ref[0:1, :] * _rms(o_ref[...], gpost_ref[...])
    xm_ref[...] = xm
    h = _rms(xm, gpre_ref[...])
    h_ref[...] = (h * (1.0 + sc_ref[0:1, :]) + sh_ref[0:1, :]).astype(h_ref.dtype)


def _post(x, o, g_post, mods, g_pre):
    s, d = x.shape
    tm = min(ROW_TILE, s)
    row = pl.BlockSpec((tm, d), lambda i: (i, 0))
    vec = pl.BlockSpec((1, d), lambda i: (0, 0))

    def mod(idx):
        return pl.BlockSpec((2, d), lambda i: (0, idx))

    return pl.pallas_call(
        _post_kernel,
        grid=(s // tm,),
        in_specs=[row, row, vec, mod(2), vec, mod(3), mod(4)],
        out_specs=[row, row],
        out_shape=[jax.ShapeDtypeStruct((s, d), F32), jax.ShapeDtypeStruct((s, d), BF16)],
        compiler_params=_params(1),
        name="post",
    )(x, o, g_post.reshape(1, d), mods, g_pre.reshape(1, d), mods, mods)


def _ffn_kernel(hp_ref, hm_ref, hn_ref, wg_ref, wu_ref, cwg_ref, cwu_ref, cbg_ref, cbu_ref, wd_ref,
                o_ref, lhs_ref, r0_ref, r1_ref, act_ref, *, nf):
    i = pl.program_id(0)
    f = pl.program_id(1)
    tm = hm_ref.shape[0]
    halo = hp_ref.shape[0]
    rows = tm + 2 * halo
    r_refs = (r0_ref, r1_ref)

    def up_proj(r_ref, which):
        w_ref = (wg_ref, wu_ref)[which]
        r_ref[which] = jnp.dot(lhs_ref[...], w_ref[...], preferred_element_type=F32)

    def conv(win, cw_ref, cb_ref):
        n = win.shape[0]
        prev = pltpu.roll(win, 1, 0)[SUBLANES:n - SUBLANES, :]
        nxt = pltpu.roll(win, n - 1, 0)[SUBLANES:n - SUBLANES, :]
        return (prev * cw_ref[0:1, :] + win[SUBLANES:n - SUBLANES, :] * cw_ref[1:2, :]
                + nxt * cw_ref[2:3, :] + cb_ref[...])

    def activation(r_ref):
        rb = min(FFN_ACT_ROWS, tm)
        for b in range(tm // rb):
            lo = halo + b * rb - SUBLANES
            gate = conv(r_ref[0, lo:lo + rb + 2 * SUBLANES, :], cwg_ref, cbg_ref)
            up = conv(r_ref[1, lo:lo + rb + 2 * SUBLANES, :], cwu_ref, cbu_ref)
            act_ref[b * rb:(b + 1) * rb, :] = (gate * jax.nn.sigmoid(gate) * up).astype(BF16)

    def down_proj():
        o_ref[...] += jnp.dot(act_ref[...], wd_ref[...], preferred_element_type=F32)

    @pl.when(f == 0)
    def _():
        lhs_ref[0:halo, :] = jnp.where(i > 0, hp_ref[...], jnp.zeros_like(hp_ref[...]))
        lhs_ref[halo:halo + tm, :] = hm_ref[...]
        lhs_ref[halo + tm:, :] = jnp.where(i < pl.num_programs(0) - 1, hn_ref[...],
                                           jnp.zeros_like(hn_ref[...]))
        o_ref[...] = jnp.zeros(o_ref.shape, F32)
        up_proj(r0_ref, 0)
        up_proj(r0_ref, 1)

    for parity in (0, 1):
        @pl.when(jnp.logical_and(jnp.logical_and(f > 0, f < nf), f % 2 == parity))
        def _(parity=parity):
            activation(r_refs[1 - parity])
            up_proj(r_refs[parity], 0)
            down_proj()
            up_proj(r_refs[parity], 1)

    @pl.when(f == nf)
    def _():
        activation(r_refs[(nf - 1) % 2])
        down_proj()


def _ffn(h, w_gu, conv_w, conv_b, w_down):
    s, d = h.shape
    ff = w_down.shape[0]
    tm = min(FFN_TM, s)
    tf = min(FFN_TF, ff)
    assert ff % tf == 0 and s % tm == 0 and tm % BF16_ROWS == 0
    nf = ff // tf
    hb = tm // BF16_ROWS
    n_hb = s // BF16_ROWS
    single = pl.Buffered(1)

    def cur(f):
        return jnp.minimum(f, nf - 1)

    def prv(f):
        return jnp.maximum(f - 1, 0)

    return pl.pallas_call(
        functools.partial(_ffn_kernel, nf=nf),
        grid=(s // tm, nf + 1),
        in_specs=[pl.BlockSpec((BF16_ROWS, d), lambda i, f: (jnp.maximum(i * hb - 1, 0), 0)),
                  pl.BlockSpec((tm, d), lambda i, f: (i, 0), pipeline_mode=single),
                  pl.BlockSpec((BF16_ROWS, d), lambda i, f: (jnp.minimum((i + 1) * hb, n_hb - 1), 0)),
                  pl.BlockSpec((d, tf), lambda i, f: (0, cur(f))),
                  pl.BlockSpec((d, tf), lambda i, f: (0, nf + cur(f))),
                  pl.BlockSpec((3, tf), lambda i, f: (0, prv(f))),
                  pl.BlockSpec((3, tf), lambda i, f: (0, nf + prv(f))),
                  pl.BlockSpec((1, tf), lambda i, f: (0, prv(f))),
                  pl.BlockSpec((1, tf), lambda i, f: (0, nf + prv(f))),
                  pl.BlockSpec((tf, d), lambda i, f: (prv(f), 0))],
        out_specs=pl.BlockSpec((tm, d), lambda i, f: (i, 0), pipeline_mode=single),
        out_shape=jax.ShapeDtypeStruct((s, d), F32),
        scratch_shapes=[pltpu.VMEM((tm + 2 * BF16_ROWS, d), BF16),
                        pltpu.VMEM((2, tm + 2 * BF16_ROWS, tf), F32),
                        pltpu.VMEM((2, tm + 2 * BF16_ROWS, tf), F32),
                        pltpu.VMEM((tm, tf), BF16)],
        compiler_params=_params(2),
        name="ffn",
    )(h, h, h, w_gu, w_gu, conv_w, conv_w, conv_b, conv_b, w_down)


def _final_kernel(xm_ref, f_ref, g_ref, gate_ref, o_ref):
    o_ref[...] = xm_ref[...] + gate_ref[0:1, :] * _rms(f_ref[...], g_ref[...])


def _final(xm, f, g_post, mods):
    s, d = xm.shape
    tm = min(ROW_TILE, s)
    row = pl.BlockSpec((tm, d), lambda i: (i, 0))
    return pl.pallas_call(
        _final_kernel,
        grid=(s // tm,),
        in_specs=[row, row, pl.BlockSpec((1, d), lambda i: (0, 0)),
                  pl.BlockSpec((2, d), lambda i: (0, 5))],
        out_specs=row,
        out_shape=jax.ShapeDtypeStruct((s, d), F32),
        compiler_params=_params(1),
        name="final",
    )(xm, f, g_post.reshape(1, d), mods)


def _rope_tables(s):
    quarter = HEAD_DIM // 4
    pos = np.arange(s)
    freqs = ROPE_THETA ** (-np.arange(quarter, dtype=np.float64) / quarter)
    ang_r = (pos // GRID_W)[:, None] * freqs[None, :]
    ang_c = (pos % GRID_W)[:, None] * freqs[None, :]
    cos_t = np.concatenate([np.cos(ang_r), np.cos(ang_r), np.cos(ang_c), np.cos(ang_c)], axis=1)
    sin_t = np.concatenate([-np.sin(ang_r), np.sin(ang_r), -np.sin(ang_c), np.sin(ang_c)], axis=1)
    return jnp.asarray(cos_t, F32), jnp.asarray(sin_t, F32)


def kernel(x, c, ctx, c_ctx, w_ada, b_ada, g_pre_mix, g_post_mix, g_pre_ffn, g_post_ffn, w_in, ln_v_g,
           ln_v_b, w_s, b_s, g_q, g_k, g_out_a, g_out_b, w_out, w_up, conv_w, conv_b, w_down):
    bsz, s, d = x.shape
    assert bsz == 1 and w_in.shape[0] == 1, "single batch element, single layer"
    aw = ln_v_g.shape[-1]
    bw = g_out_b.shape[-1]
    in_cols = w_in.shape[-1]
    kvw = (in_cols - 2 * aw - bw) // 2
    n_heads, n_kv = bw // HEAD_DIM, kvw // HEAD_DIM
    q0, k0, v0 = 2 * aw, 2 * aw + bw, 2 * aw + bw + kvw
    ff = w_down.shape[1]

    x2 = x[0]
    ctx2 = ctx[0]
    w_in_b = w_in[0].astype(BF16)

    mods = _adaln(jnp.concatenate([c, c_ctx[None, :]], axis=0), w_ada[0], b_ada[0])

    cos_t, sin_t = _rope_tables(s)
    h = _prenorm(x2, g_pre_mix[0], mods, 0, 0, 1)
    hc = _prenorm(ctx2, g_pre_mix[0], mods, 1, 0, 1)

    kinds = (["gelu"] * (2 * aw // LANES) + ["q"] * n_heads + ["k"] * n_kv + ["v"] * n_kv)
    z, w_gu_b, w_out_b = _inproj(h, w_in_b, 0, kinds, cos_t, sin_t, g_q[0], g_k[0],
                                 side=(w_up[0], w_out[0]))
    zc, = _inproj(hc, w_in_b, k0, ["kc"] * n_kv + ["v"] * n_kv, cos_t, sin_t, g_q[0], g_k[0])

    out_a = _gmlp(z, 2 * aw, ln_v_g[0], ln_v_b[0], w_s[0], b_s[0], g_out_a[0])
    out_b = _attention(z, q0, k0, v0, n_heads, n_kv, zc, g_out_b[0])
    o, w_down_b = _outproj(out_a, out_b, w_out_b, side=(w_down[0],))
    x_mix, h2 = _post(x2, o, g_post_mix[0], mods, g_pre_ffn[0])
    f = _ffn(h2, w_gu_b, conv_w[0], conv_b[0].reshape(1, 2 * ff), w_down_b)
    out = _final(x_mix, f, g_post_ffn[0], mods)
    return out[None]
```

```python
import functools

import numpy as np
import jax
import jax.numpy as jnp
from jax import lax
from jax.experimental import pallas as pl
from jax.experimental.pallas import tpu as pltpu

GRID_W = 64
HEAD_DIM = 128
ROPE_THETA = 10000.0
EPS = 1e-6
N_MOD = 6
LOG2_E = 1.4426950408889634

LANES = 128
SUBLANES = 8
BF16_ROWS = 16
VMEM_LIMIT_BYTES = 56 * 1024 * 1024

ADALN_TN = 1024
ROW_TILE = 256
INPROJ_TM = 512
INPROJ_TN = 1024
GMLP_CHUNKS_PER_STEP = 2
ATTN_TQ = 256
ATTN_TK = 1024
ATTN_SCORE_AHEAD = 2
ATTN_VALUE_LAG = 2
OUTPROJ_TM = 512
OUTPROJ_TN = 1024
FFN_TM = 1024
FFN_TF = 256
FFN_ACT_ROWS = 256

F32 = jnp.float32
BF16 = jnp.bfloat16


def _params(n_axes):
    return pltpu.CompilerParams(dimension_semantics=("arbitrary",) * n_axes,
                                vmem_limit_bytes=VMEM_LIMIT_BYTES)


def _rms(x, g):
    return x * lax.rsqrt(jnp.mean(x * x, axis=-1, keepdims=True) + EPS) * g


def _adaln_kernel(c_ref, w_ref, b_ref, o_ref, s_ref):
    k_dim, tn = w_ref.shape
    nj = tn // LANES

    @pl.when(pl.program_id(0) == 0)
    def _():
        c = c_ref[...]
        s_ref[...] = c * jax.nn.sigmoid(c)

    def body(r, accs):
        rows = pl.ds(pl.multiple_of(r * SUBLANES, SUBLANES), SUBLANES)
        s0 = s_ref[0, rows, :]
        s1 = s_ref[1, rows, :]
        new = []
        for j in range(nj):
            w = w_ref[rows, j * LANES:(j + 1) * LANES]
            new.append(accs[2 * j] + w * s0)
            new.append(accs[2 * j + 1] + w * s1)
        return tuple(new)

    zero = jnp.zeros((SUBLANES, LANES), F32)
    accs = lax.fori_loop(0, k_dim // SUBLANES, body, (zero,) * (2 * nj), unroll=4)
    for j in range(nj):
        cols = slice(j * LANES, (j + 1) * LANES)
        o_ref[0:1, cols] = jnp.sum(accs[2 * j], axis=0, keepdims=True) + b_ref[:, cols]
        o_ref[1:2, cols] = jnp.sum(accs[2 * j + 1], axis=0, keepdims=True) + b_ref[:, cols]


def _adaln(cc, w_ada, b_ada):
    k_dim, n = w_ada.shape
    tn = min(ADALN_TN, n)
    c_rep = jnp.broadcast_to(cc[:, :, None], (2, k_dim, LANES))
    return pl.pallas_call(
        _adaln_kernel,
        grid=(n // tn,),
        in_specs=[pl.BlockSpec((2, k_dim, LANES), lambda j: (0, 0, 0)),
                  pl.BlockSpec((k_dim, tn), lambda j: (0, j)),
                  pl.BlockSpec((1, tn), lambda j: (0, j))],
        out_specs=pl.BlockSpec((2, tn), lambda j: (0, j)),
        out_shape=jax.ShapeDtypeStruct((2, n), F32),
        scratch_shapes=[pltpu.VMEM((2, k_dim, LANES), F32)],
        compiler_params=_params(1),
        name="adaln",
    )(c_rep, w_ada, b_ada.reshape(1, n))


def _prenorm_kernel(x_ref, g_ref, sh_ref, sc_ref, o_ref, *, row):
    y = _rms(x_ref[...], g_ref[...])
    o_ref[...] = (y * (1.0 + sc_ref[row:row + 1, :]) + sh_ref[row:row + 1, :]).astype(o_ref.dtype)


def _prenorm(x, g, mods, row, shift_idx, scale_idx):
    s, d = x.shape
    tm = min(ROW_TILE, s)
    return pl.pallas_call(
        functools.partial(_prenorm_kernel, row=row),
        grid=(s // tm,),
        in_specs=[pl.BlockSpec((tm, d), lambda i: (i, 0)),
                  pl.BlockSpec((1, d), lambda i: (0, 0)),
                  pl.BlockSpec((2, d), lambda i: (0, shift_idx)),
                  pl.BlockSpec((2, d), lambda i: (0, scale_idx))],
        out_specs=pl.BlockSpec((tm, d), lambda i: (i, 0)),
        out_shape=jax.ShapeDtypeStruct((s, d), BF16),
        compiler_params=_params(1),
        name="prenorm",
    )(x, g.reshape(1, d), mods, mods)


def _swap_rope_halves(a):
    lane = lax.broadcasted_iota(jnp.int32, a.shape, 1)
    return jnp.where((lane & 32) != 0, pltpu.roll(a, 32, 1), pltpu.roll(a, LANES - 32, 1))


def _head_epilogue(a, kind, cos, sin, gq, gk):
    if kind == "gelu":
        return 0.5 * a * (1.0 + lax.erf(a * (2.0 ** -0.5)))
    if kind == "v":
        return a
    a = _rms(a, gq if kind == "q" else gk)
    if kind != "kc":
        a = a * cos + _swap_rope_halves(a) * sin
    if kind == "q":
        a = a * (HEAD_DIM ** -0.5 * LOG2_E)
    return a


def _side_cast_specs(arrays, n_steps, n_inner):
    in_specs, out_specs, out_shapes, counts = [], [], [], []
    for a in arrays:
        rows, cols = a.shape
        rb = next(r for r in range(BF16_ROWS, rows + 1, BF16_ROWS)
                  if rows % r == 0 and rows // r <= n_steps)
        nb = rows // rb

        def index(j, i, nb=nb):
            return (jnp.minimum(j * n_inner + i, nb - 1), 0)

        in_specs.append(pl.BlockSpec((rb, cols), index))
        out_specs.append(pl.BlockSpec((rb, cols), index))
        out_shapes.append(jax.ShapeDtypeStruct((rows, cols), BF16))
        counts.append(nb)
    return in_specs, out_specs, out_shapes, tuple(counts)


def _side_cast(src_refs, dst_refs, counts):
    step = pl.program_id(0) * pl.num_programs(1) + pl.program_id(1)
    for src, dst, nb in zip(src_refs, dst_refs, counts):
        @pl.when(step < nb)
        def _(src=src, dst=dst):
            dst[...] = src[...].astype(dst.dtype)


def _inproj_kernel(h_ref, w_ref, cos_ref, sin_ref, gq_ref, gk_ref, *refs, plans, side_counts):
    n_side = len(side_counts)
    o_ref = refs[n_side]
    _side_cast(refs[:n_side], refs[n_side + 1:], side_counts)
    j = pl.program_id(0)
    for kinds, tiles in plans:
        cond = functools.reduce(jnp.logical_or, [j == t for t in tiles])

        @pl.when(cond)
        def _(kinds=kinds):
            acc = jnp.dot(h_ref[...], w_ref[...], preferred_element_type=F32)
            for c, kind in enumerate(kinds):
                cols = slice(c * LANES, (c + 1) * LANES)
                o_ref[:, cols] = _head_epilogue(acc[:, cols], kind, cos_ref[...], sin_ref[...],
                                                gq_ref[...], gk_ref[...]).astype(o_ref.dtype)


def _inproj(h, w, col0, kinds_per_chunk, cos_t, sin_t, gq, gk, side=()):
    s, d = h.shape
    n = LANES * len(kinds_per_chunk)
    tm = min(INPROJ_TM, s)
    tn = min(INPROJ_TN, n)
    assert n % tn == 0 and col0 % tn == 0 and s % tm == 0
    per_tile = tn // LANES
    by_kinds = {}
    for t in range(n // tn):
        by_kinds.setdefault(tuple(kinds_per_chunk[t * per_tile:(t + 1) * per_tile]), []).append(t)
    plans = tuple((k, tuple(v)) for k, v in by_kinds.items())
    j0 = col0 // tn
    grid = (n // tn, s // tm)
    side_in, side_out, side_shapes, side_counts = _side_cast_specs(side, grid[0] * grid[1], grid[1])
    return pl.pallas_call(
        functools.partial(_inproj_kernel, plans=plans, side_counts=side_counts),
        grid=grid,
        in_specs=[pl.BlockSpec((tm, d), lambda j, i: (i, 0)),
                  pl.BlockSpec((d, tn), lambda j, i: (0, j + j0)),
                  pl.BlockSpec((tm, LANES), lambda j, i: (i, 0)),
                  pl.BlockSpec((tm, LANES), lambda j, i: (i, 0)),
                  pl.BlockSpec((1, LANES), lambda j, i: (0, 0)),
                  pl.BlockSpec((1, LANES), lambda j, i: (0, 0))] + side_in,
        out_specs=[pl.BlockSpec((tm, tn), lambda j, i: (i, j))] + side_out,
        out_shape=[jax.ShapeDtypeStruct((s, n), BF16)] + side_shapes,
        compiler_params=_params(2),
        name="inproj",
    )(h, w, cos_t, sin_t, gq.reshape(1, LANES), gk.reshape(1, LANES), *side)


def _gmlp_kernel(z_ref, lng_ref, lnb_ref, ws_ref, bs_ref, go_ref, o_ref, oa_ref, *, chunk, groups):
    aw = groups * LANES
    for cidx in range(z_ref.shape[0] // chunk):
        rows = slice(cidx * chunk, (cidx + 1) * chunk)
        v = z_ref[rows, aw:].astype(F32)
        mu = jnp.mean(v, axis=-1, keepdims=True)
        vc = v - mu
        var = jnp.mean(vc * vc, axis=-1, keepdims=True)
        vn = (vc * lax.rsqrt(var + EPS) * lng_ref[...] + lnb_ref[...]).astype(BF16)
        for g in range(groups):
            cols = slice(g * LANES, (g + 1) * LANES)
            mixed = jnp.dot(ws_ref[g], vn[:, cols], preferred_element_type=F32) + bs_ref[g]
            oa_ref[:, cols] = z_ref[rows, cols].astype(F32) * mixed
        o_ref[rows, :] = _rms(oa_ref[...], go_ref[...]).astype(o_ref.dtype)


def _gmlp(z, n_cols_a, ln_g, ln_b, w_s, b_s, g_out):
    s = z.shape[0]
    groups, chunk, _ = w_s.shape
    aw = n_cols_a // 2
    assert aw == groups * LANES
    rows = chunk * min(GMLP_CHUNKS_PER_STEP, s // chunk)
    bs_rep = jnp.broadcast_to(b_s[:, :, None], (groups, chunk, LANES))
    return pl.pallas_call(
        functools.partial(_gmlp_kernel, chunk=chunk, groups=groups),
        grid=(s // rows,),
        in_specs=[pl.BlockSpec((rows, 2 * aw), lambda i: (i, 0)),
                  pl.BlockSpec((1, aw), lambda i: (0, 0)),
                  pl.BlockSpec((1, aw), lambda i: (0, 0)),
                  pl.BlockSpec((groups, chunk, chunk), lambda i: (0, 0, 0)),
                  pl.BlockSpec((groups, chunk, LANES), lambda i: (0, 0, 0)),
                  pl.BlockSpec((1, aw), lambda i: (0, 0))],
        out_specs=pl.BlockSpec((rows, aw), lambda i: (i, 0)),
        out_shape=jax.ShapeDtypeStruct((s, aw), BF16),
        scratch_shapes=[pltpu.VMEM((chunk, aw), F32)],
        compiler_params=_params(1),
        name="gmlp",
    )(z, ln_g.reshape(1, aw), ln_b.reshape(1, aw), w_s.astype(BF16), bs_rep, g_out.reshape(1, aw))


def _attn_kernel(q_ref, kc_ref, vc_ref, k_ref, v_ref, go_ref, o_ref, ob_ref, *, group, tk):
    kh = pl.program_id(1)
    n_kh = pl.num_programs(1)

    chunks = [(kc_ref, 0, kc_ref.shape[0])]
    chunks += [(k_ref, c * tk, tk) for c in range(k_ref.shape[0] // tk)]
    v_refs = {id(kc_ref): vc_ref, id(k_ref): v_ref}
    tiles = [(ci, g) for ci in range(len(chunks)) for g in range(group)]

    def scores(tile):
        ci, g = tile
        ref, lo, n = chunks[ci]
        q = q_ref[:, g * HEAD_DIM:(g + 1) * HEAD_DIM]
        return lax.dot_general(ref[lo:lo + n, :], q, (((1,), (1,)), ((), ())),
                               preferred_element_type=F32)

    stats = [None] * group
    accs = [None] * group
    v_t = {}

    def softmax(tile, s):
        g = tile[1]
        m_blk = jnp.max(s, axis=0, keepdims=True)
        if stats[g] is None:
            p = jnp.exp2(s - m_blk)
            stats[g] = (m_blk, jnp.sum(p, axis=0, keepdims=True))
            return tile, p.astype(BF16), None
        m_old, l_old = stats[g]
        m_new = jnp.maximum(m_old, m_blk)
        p = jnp.exp2(s - m_new)
        alpha = jnp.exp2(m_old - m_new)
        stats[g] = (m_new, alpha * l_old + jnp.sum(p, axis=0, keepdims=True))
        return tile, p.astype(BF16), alpha

    def values(pending):
        (ci, g), p, alpha = pending
        if ci not in v_t:
            ref, lo, n = chunks[ci]
            v_t[ci] = v_refs[id(ref)][lo:lo + n, :].T
        pv = jnp.dot(v_t[ci], p, preferred_element_type=F32)
        accs[g] = pv if alpha is None else alpha * accs[g] + pv

    ready = [scores(tiles[t]) for t in range(min(ATTN_SCORE_AHEAD, len(tiles)))]
    pending = []
    for t, tile in enumerate(tiles):
        if t + ATTN_SCORE_AHEAD < len(tiles):
            ready.append(scores(tiles[t + ATTN_SCORE_AHEAD]))
        pending.append(softmax(tile, ready.pop(0)))
        if len(pending) > ATTN_VALUE_LAG:
            values(pending.pop(0))
    for item in pending:
        values(item)

    for g in range(group):
        ob_ref[kh * group + g] = (accs[g] / stats[g][1]).T

    @pl.when(kh == n_kh - 1)
    def _():
        n_heads = ob_ref.shape[0]
        ssq = jnp.zeros((q_ref.shape[0], 1), F32)
        for h in range(n_heads):
            o = ob_ref[h]
            ssq = ssq + jnp.sum(o * o, axis=-1, keepdims=True)
        inv = lax.rsqrt(ssq / (n_heads * HEAD_DIM) + EPS)
        for h in range(n_heads):
            cols = slice(h * HEAD_DIM, (h + 1) * HEAD_DIM)
            o_ref[:, cols] = (ob_ref[h] * inv * go_ref[:, cols]).astype(o_ref.dtype)


def _attention(z, q_col0, k_col0, v_col0, n_heads, n_kv, zc, g_out):
    s = z.shape[0]
    c = zc.shape[0]
    group = n_heads // n_kv
    gw = group * HEAD_DIM
    bw = n_heads * HEAD_DIM
    tq = min(ATTN_TQ, s)
    tk = min(ATTN_TK, s)
    assert q_col0 % gw == 0 and k_col0 % HEAD_DIM == 0 and v_col0 % HEAD_DIM == 0
    qb, kb, vb = q_col0 // gw, k_col0 // HEAD_DIM, v_col0 // HEAD_DIM
    return pl.pallas_call(
        functools.partial(_attn_kernel, group=group, tk=tk),
        grid=(s // tq, n_kv),
        in_specs=[pl.BlockSpec((tq, gw), lambda i, kh: (i, qb + kh)),
                  pl.BlockSpec((c, HEAD_DIM), lambda i, kh: (0, kh)),
                  pl.BlockSpec((c, HEAD_DIM), lambda i, kh: (0, n_kv + kh)),
                  pl.BlockSpec((s, HEAD_DIM), lambda i, kh: (0, kb + kh)),
                  pl.BlockSpec((s, HEAD_DIM), lambda i, kh: (0, vb + kh)),
                  pl.BlockSpec((1, bw), lambda i, kh: (0, 0))],
        out_specs=pl.BlockSpec((tq, bw), lambda i, kh: (i, 0)),
        out_shape=jax.ShapeDtypeStruct((s, bw), BF16),
        scratch_shapes=[pltpu.VMEM((n_heads, tq, HEAD_DIM), F32)],
        compiler_params=_params(2),
        name="attn",
    )(z, zc, zc, z, z, g_out.reshape(1, bw))


def _outproj_kernel(a_ref, b_ref, wa_ref, wb_ref, *refs, side_counts):
    n_side = len(side_counts)
    o_ref = refs[n_side]
    _side_cast(refs[:n_side], refs[n_side + 1:], side_counts)
    o_ref[...] = (jnp.dot(a_ref[...], wa_ref[...], preferred_element_type=F32)
                  + jnp.dot(b_ref[...], wb_ref[...], preferred_element_type=F32))


def _outproj(a, b, w, side=()):
    s, aw = a.shape
    bw = b.shape[1]
    n = w.shape[1]
    tm = min(OUTPROJ_TM, s)
    tn = min(OUTPROJ_TN, n)
    assert aw == bw
    grid = (n // tn, s // tm)
    side_in, side_out, side_shapes, side_counts = _side_cast_specs(side, grid[0] * grid[1], grid[1])
    return pl.pallas_call(
        functools.partial(_outproj_kernel, side_counts=side_counts),
        grid=grid,
        in_specs=[pl.BlockSpec((tm, aw), lambda j, i: (i, 0)),
                  pl.BlockSpec((tm, bw), lambda j, i: (i, 0)),
                  pl.BlockSpec((aw, tn), lambda j, i: (0, j)),
                  pl.BlockSpec((bw, tn), lambda j, i: (1, j))] + side_in,
        out_specs=[pl.BlockSpec((tm, tn), lambda j, i: (i, j))] + side_out,
        out_shape=[jax.ShapeDtypeStruct((s, n), F32)] + side_shapes,
        compiler_params=_params(2),
        name="outproj",
    )(a, b, w, w, *side)


def _post_kernel(x_ref, o_ref, gpost_ref, gate_ref, gpre_ref, sh_ref, sc_ref, xm_ref, h_ref):
    xm = x_ref[...] + gate_ref[0:1, :] * _rms(o_ref[...], gpost_ref[...])
    xm_ref[...] = xm
    h = _rms(xm, gpre_ref[...])
    h_ref[...] = (h * (1.0 + sc_ref[0:1, :]) + sh_ref[0:1, :]).astype(h_ref.dtype)


def _post(x, o, g_post, mods, g_pre):
    s, d = x.shape
    tm = min(ROW_TILE, s)
    row = pl.BlockSpec((tm, d), lambda i: (i, 0))
    vec = pl.BlockSpec((1, d), lambda i: (0, 0))

    def mod(idx):
        return pl.BlockSpec((2, d), lambda i: (0, idx))

    return pl.pallas_call(
        _post_kernel,
        grid=(s // tm,),
        in_specs=[row, row, vec, mod(2), vec, mod(3), mod(4)],
        out_specs=[row, row],
        out_shape=[jax.ShapeDtypeStruct((s, d), F32), jax.ShapeDtypeStruct((s, d), BF16)],
        compiler_params=_params(1),
        name="post",
    )(x, o, g_post.reshape(1, d), mods, g_pre.reshape(1, d), mods, mods)


def _ffn_kernel(hp_ref, hm_ref, hn_ref, wg_ref, wu_ref, cwg_ref, cwu_ref, cbg_ref, cbu_ref, wd_ref,
                o_ref, lhs_ref, r0_ref, r1_ref, act_ref, *, nf):
    i = pl.program_id(0)
    f = pl.program_id(1)
    tm = hm_ref.shape[0]
    halo = hp_ref.shape[0]
    rows = tm + 2 * halo
    r_refs = (r0_ref, r1_ref)

    def up_proj(r_ref, which):
        w_ref = (wg_ref, wu_ref)[which]
        r_ref[which] = jnp.dot(lhs_ref[...], w_ref[...], preferred_element_type=F32)

    def conv(win, cw_ref, cb_ref):
        n = win.shape[0]
        prev = pltpu.roll(win, 1, 0)[SUBLANES:n - SUBLANES, :]
        nxt = pltpu.roll(win, n - 1, 0)[SUBLANES:n - SUBLANES, :]
        return (prev * cw_ref[0:1, :] + win[SUBLANES:n - SUBLANES, :] * cw_ref[1:2, :]
                + nxt * cw_ref[2:3, :] + cb_ref[...])

    def activation(r_ref):
        rb = min(FFN_ACT_ROWS, tm)
        for b in range(tm // rb):
            lo = halo + b * rb - SUBLANES
            gate = conv(r_ref[0, lo:lo + rb + 2 * SUBLANES, :], cwg_ref, cbg_ref)
            up = conv(r_ref[1, lo:lo + rb + 2 * SUBLANES, :], cwu_ref, cbu_ref)
            act_ref[b * rb:(b + 1) * rb, :] = (gate * jax.nn.sigmoid(gate) * up).astype(BF16)

    def down_proj():
        o_ref[...] += jnp.dot(act_ref[...], wd_ref[...], preferred_element_type=F32)

    @pl.when(f == 0)
    def _():
        lhs_ref[0:halo, :] = jnp.where(i > 0, hp_ref[...], jnp.zeros_like(hp_ref[...]))
        lhs_ref[halo:halo + tm, :] = hm_ref[...]
        lhs_ref[halo + tm:, :] = jnp.where(i < pl.num_programs(0) - 1, hn_ref[...],
                                           jnp.zeros_like(hn_ref[...]))
        o_ref[...] = jnp.zeros(o_ref.shape, F32)
        up_proj(r0_ref, 0)
        up_proj(r0_ref, 1)

    for parity in (0, 1):
        @pl.when(jnp.logical_and(jnp.logical_and(f > 0, f < nf), f % 2 == parity))
        def _(parity=parity):
            activation(r_refs[1 - parity])
            up_proj(r_refs[parity], 0)
            down_proj()
            up_proj(r_refs[parity], 1)

    @pl.when(f == nf)
    def _():
        activation(r_refs[(nf - 1) % 2])
        down_proj()


def _ffn(h, w_gu, conv_w, conv_b, w_down):
    s, d = h.shape
    ff = w_down.shape[0]
    tm = min(FFN_TM, s)
    tf = min(FFN_TF, ff)
    assert ff % tf == 0 and s % tm == 0 and tm % BF16_ROWS == 0
    nf = ff // tf
    hb = tm // BF16_ROWS
    n_hb = s // BF16_ROWS
    single = pl.Buffered(1)

    def cur(f):
        return jnp.minimum(f, nf - 1)

    def prv(f):
        return jnp.maximum(f - 1, 0)

    return pl.pallas_call(
        functools.partial(_ffn_kernel, nf=nf),
        grid=(s // tm, nf + 1),
        in_specs=[pl.BlockSpec((BF16_ROWS, d), lambda i, f: (jnp.maximum(i * hb - 1, 0), 0)),
                  pl.BlockSpec((tm, d), lambda i, f: (i, 0), pipeline_mode=single),
                  pl.BlockSpec((BF16_ROWS, d), lambda i, f: (jnp.minimum((i + 1) * hb, n_hb - 1), 0)),
                  pl.BlockSpec((d, tf), lambda i, f: (0, cur(f))),
                  pl.BlockSpec((d, tf), lambda i, f: (0, nf + cur(f))),
                  pl.BlockSpec((3, tf), lambda i, f: (0, prv(f))),
                  pl.BlockSpec((3, tf), lambda i, f: (0, nf + prv(f))),
                  pl.BlockSpec((1, tf), lambda i, f: (0, prv(f))),
                  pl.BlockSpec((1, tf), lambda i, f: (0, nf + prv(f))),
                  pl.BlockSpec((tf, d), lambda i, f: (prv(f), 0))],
        out_specs=pl.BlockSpec((tm, d), lambda i, f: (i, 0), pipeline_mode=single),
        out_shape=jax.ShapeDtypeStruct((s, d), F32),
        scratch_shapes=[pltpu.VMEM((tm + 2 * BF16_ROWS, d), BF16),
                        pltpu.VMEM((2, tm + 2 * BF16_ROWS, tf), F32),
                        pltpu.VMEM((2, tm + 2 * BF16_ROWS, tf), F32),
                        pltpu.VMEM((tm, tf), BF16)],
        compiler_params=_params(2),
        name="ffn",
    )(h, h, h, w_gu, w_gu, conv_w, conv_w, conv_b, conv_b, w_down)


def _final_kernel(xm_ref, f_ref, g_ref, gate_ref, o_ref):
    o_ref[...] = xm_ref[...] + gate_ref[0:1, :] * _rms(f_ref[...], g_ref[...])


def _final(xm, f, g_post, mods):
    s, d = xm.shape
    tm = min(ROW_TILE, s)
    row = pl.BlockSpec((tm, d), lambda i: (i, 0))
    return pl.pallas_call(
        _final_kernel,
        grid=(s // tm,),
        in_specs=[row, row, pl.BlockSpec((1, d), lambda i: (0, 0)),
                  pl.BlockSpec((2, d), lambda i: (0, 5))],
        out_specs=row,
        out_shape=jax.ShapeDtypeStruct((s, d), F32),
        compiler_params=_params(1),
        name="final",
    )(xm, f, g_post.reshape(1, d), mods)


def _rope_tables(s):
    quarter = HEAD_DIM // 4
    pos = np.arange(s)
    freqs = ROPE_THETA ** (-np.arange(quarter, dtype=np.float64) / quarter)
    ang_r = (pos // GRID_W)[:, None] * freqs[None, :]
    ang_c = (pos % GRID_W)[:, None] * freqs[None, :]
    cos_t = np.concatenate([np.cos(ang_r), np.cos(ang_r), np.cos(ang_c), np.cos(ang_c)], axis=1)
    sin_t = np.concatenate([-np.sin(ang_r), np.sin(ang_r), -np.sin(ang_c), np.sin(ang_c)], axis=1)
    return jnp.asarray(cos_t, F32), jnp.asarray(sin_t, F32)


def kernel(x, c, ctx, c_ctx, w_ada, b_ada, g_pre_mix, g_post_mix, g_pre_ffn, g_post_ffn, w_in, ln_v_g,
           ln_v_b, w_s, b_s, g_q, g_k, g_out_a, g_out_b, w_out, w_up, conv_w, conv_b, w_down):
    bsz, s, d = x.shape
    assert bsz == 1 and w_in.shape[0] == 1, "single batch element, single layer"
    aw = ln_v_g.shape[-1]
    bw = g_out_b.shape[-1]
    in_cols = w_in.shape[-1]
    kvw = (in_cols - 2 * aw - bw) // 2
    n_heads, n_kv = bw // HEAD_DIM, kvw // HEAD_DIM
    q0, k0, v0 = 2 * aw, 2 * aw + bw, 2 * aw + bw + kvw
    ff = w_down.shape[1]

    x2 = x[0]
    ctx2 = ctx[0]
    w_in_b = w_in[0].astype(BF16)

    mods = _adaln(jnp.concatenate([c, c_ctx[None, :]], axis=0), w_ada[0], b_ada[0])

    cos_t, sin_t = _rope_tables(s)
    h = _prenorm(x2, g_pre_mix[0], mods, 0, 0, 1)
    hc = _prenorm(ctx2, g_pre_mix[0], mods, 1, 0, 1)

    kinds = (["gelu"] * (2 * aw // LANES) + ["q"] * n_heads + ["k"] * n_kv + ["v"] * n_kv)
    z, w_gu_b, w_out_b = _inproj(h, w_in_b, 0, kinds, cos_t, sin_t, g_q[0], g_k[0],
                                 side=(w_up[0], w_out[0]))
    zc, = _inproj(hc, w_in_b, k0, ["kc"] * n_kv + ["v"] * n_kv, cos_t, sin_t, g_q[0], g_k[0])

    out_a = _gmlp(z, 2 * aw, ln_v_g[0], ln_v_b[0], w_s[0], b_s[0], g_out_a[0])
    out_b = _attention(z, q0, k0, v0, n_heads, n_kv, zc, g_out_b[0])
    o, w_down_b = _outproj(out_a, out_b, w_out_b, side=(w_down[0],))
    x_mix, h2 = _post(x2, o, g_post_mix[0], mods, g_pre_ffn[0])
    f = _ffn(h2, w_gu_b, conv_w[0], conv_b[0].reshape(1, 2 * ff), w_down_b)
    out = _final(x_mix, f, g_post_ffn[0], mods)
    return out[None]
```

```python
import functools

import numpy as np
import jax
import jax.numpy as jnp
from jax import lax
from jax.experimental import pallas as pl
from jax.experimental.pallas import tpu as pltpu

GRID_W = 64
HEAD_DIM = 128
ROPE_THETA = 10000.0
EPS = 1e-6
N_MOD = 6
LOG2_E = 1.4426950408889634

LANES = 128
SUBLANES = 8
BF16_ROWS = 16
VMEM_LIMIT_BYTES = 56 * 1024 * 1024

ADALN_TN = 1024
ROW_TILE = 256
INPROJ_TM = 512
INPROJ_TN = 1024
GMLP_CHUNKS_PER_STEP = 2
ATTN_TQ = 256
ATTN_TK = 1024
ATTN_SCORE_AHEAD = 2
ATTN_VALUE_LAG = 2
OUTPROJ_TM = 512
OUTPROJ_TN = 1024
FFN_TM = 1024
FFN_TF = 256
FFN_ACT_ROWS = 256

F32 = jnp.float32
BF16 = jnp.bfloat16


def _params(n_axes):
    return pltpu.CompilerParams(dimension_semantics=("arbitrary",) * n_axes,
                                vmem_limit_bytes=VMEM_LIMIT_BYTES)


def _rms(x, g):
    return x * lax.rsqrt(jnp.mean(x * x, axis=-1, keepdims=True) + EPS) * g


def _adaln_kernel(c_ref, w_ref, b_ref, o_ref, s_ref):
    k_dim, tn = w_ref.shape
    nj = tn // LANES

    @pl.when(pl.program_id(0) == 0)
    def _():
        c = c_ref[...]
        s_ref[...] = c * jax.nn.sigmoid(c)

    def body(r, accs):
        rows = pl.ds(pl.multiple_of(r * SUBLANES, SUBLANES), SUBLANES)
        s0 = s_ref[0, rows, :]
        s1 = s_ref[1, rows, :]
        new = []
        for j in range(nj):
            w = w_ref[rows, j * LANES:(j + 1) * LANES]
            new.append(accs[2 * j] + w * s0)
            new.append(accs[2 * j + 1] + w * s1)
        return tuple(new)

    zero = jnp.zeros((SUBLANES, LANES), F32)
    accs = lax.fori_loop(0, k_dim // SUBLANES, body, (zero,) * (2 * nj), unroll=4)
    for j in range(nj):
        cols = slice(j * LANES, (j + 1) * LANES)
        o_ref[0:1, cols] = jnp.sum(accs[2 * j], axis=0, keepdims=True) + b_ref[:, cols]
        o_ref[1:2, cols] = jnp.sum(accs[2 * j + 1], axis=0, keepdims=True) + b_ref[:, cols]


def _adaln(cc, w_ada, b_ada):
    k_dim, n = w_ada.shape
    tn = min(ADALN_TN, n)
    c_rep = jnp.broadcast_to(cc[:, :, None], (2, k_dim, LANES))
    return pl.pallas_call(
        _adaln_kernel,
        grid=(n // tn,),
        in_specs=[pl.BlockSpec((2, k_dim, LANES), lambda j: (0, 0, 0)),
                  pl.BlockSpec((k_dim, tn), lambda j: (0, j)),
                  pl.BlockSpec((1, tn), lambda j: (0, j))],
        out_specs=pl.BlockSpec((2, tn), lambda j: (0, j)),
        out_shape=jax.ShapeDtypeStruct((2, n), F32),
        scratch_shapes=[pltpu.VMEM((2, k_dim, LANES), F32)],
        compiler_params=_params(1),
        name="adaln",
    )(c_rep, w_ada, b_ada.reshape(1, n))


def _prenorm_kernel(x_ref, g_ref, sh_ref, sc_ref, o_ref, *, row):
    y = _rms(x_ref[...], g_ref[...])
    o_ref[...] = (y * (1.0 + sc_ref[row:row + 1, :]) + sh_ref[row:row + 1, :]).astype(o_ref.dtype)


def _prenorm(x, g, mods, row, shift_idx, scale_idx):
    s, d = x.shape
    tm = min(ROW_TILE, s)
    return pl.pallas_call(
        functools.partial(_prenorm_kernel, row=row),
        grid=(s // tm,),
        in_specs=[pl.BlockSpec((tm, d), lambda i: (i, 0)),
                  pl.BlockSpec((1, d), lambda i: (0, 0)),
                  pl.BlockSpec((2, d), lambda i: (0, shift_idx)),
                  pl.BlockSpec((2, d), lambda i: (0, scale_idx))],
        out_specs=pl.BlockSpec((tm, d), lambda i: (i, 0)),
        out_shape=jax.ShapeDtypeStruct((s, d), BF16),
        compiler_params=_params(1),
        name="prenorm",
    )(x, g.reshape(1, d), mods, mods)


def _swap_rope_halves(a):
    lane = lax.broadcasted_iota(jnp.int32, a.shape, 1)
    return jnp.where((lane & 32) != 0, pltpu.roll(a, 32, 1), pltpu.roll(a, LANES - 32, 1))


def _head_epilogue(a, kind, cos, sin, gq, gk):
    if kind == "gelu":
        return 0.5 * a * (1.0 + lax.erf(a * (2.0 ** -0.5)))
    if kind == "v":
        return a
    a = _rms(a, gq if kind == "q" else gk)
    if kind != "kc":
        a = a * cos + _swap_rope_halves(a) * sin
    if kind == "q":
        a = a * (HEAD_DIM ** -0.5 * LOG2_E)
    return a


def _side_cast_specs(side, n_steps, n_inner):
    in_specs, out_specs, out_shapes, counts = [], [], [], []
    for a, pair in side:
        rows, cols = a.shape
        rb = next(r for r in range(BF16_ROWS, rows + 1, BF16_ROWS)
                  if rows % r == 0 and rows // r <= n_steps)
        nb = rows // rb

        def index(j, i, nb=nb):
            return (jnp.minimum(j * n_inner + i, nb - 1), 0)

        in_specs.append(pl.BlockSpec((rb, cols), index))
        out_specs.append(pl.BlockSpec((rb, cols), index))
        out_shapes.append(jax.ShapeDtypeStruct((rows, cols), BF16))
        counts.append((nb, pair))
    return in_specs, out_specs, out_shapes, tuple(counts)


def _side_cast(src_refs, dst_refs, plans):
    step = pl.program_id(0) * pl.num_programs(1) + pl.program_id(1)
    for src, dst, (nb, pair) in zip(src_refs, dst_refs, plans):
        @pl.when(step < nb)
        def _(src=src, dst=dst, pair=pair):
            if not pair:
                dst[...] = src[...].astype(dst.dtype)
                return
            half = src.shape[1] // 2
            for t in range(half // pair):
                lo = 2 * t * pair
                dst[:, lo:lo + pair] = src[:, t * pair:(t + 1) * pair].astype(dst.dtype)
                dst[:, lo + pair:lo + 2 * pair] = (
                    src[:, half + t * pair:half + (t + 1) * pair].astype(dst.dtype))


def _inproj_kernel(h_ref, w_ref, cos_ref, sin_ref, gq_ref, gk_ref, *refs, plans, side_counts):
    n_side = len(side_counts)
    o_ref = refs[n_side]
    _side_cast(refs[:n_side], refs[n_side + 1:], side_counts)
    j = pl.program_id(0)
    for kinds, tiles in plans:
        cond = functools.reduce(jnp.logical_or, [j == t for t in tiles])

        @pl.when(cond)
        def _(kinds=kinds):
            acc = jnp.dot(h_ref[...], w_ref[...], preferred_element_type=F32)
            for c, kind in enumerate(kinds):
                cols = slice(c * LANES, (c + 1) * LANES)
                o_ref[:, cols] = _head_epilogue(acc[:, cols], kind, cos_ref[...], sin_ref[...],
                                                gq_ref[...], gk_ref[...]).astype(o_ref.dtype)


def _inproj(h, w, col0, kinds_per_chunk, cos_t, sin_t, gq, gk, side=()):
    s, d = h.shape
    n = LANES * len(kinds_per_chunk)
    tm = min(INPROJ_TM, s)
    tn = min(INPROJ_TN, n)
    assert n % tn == 0 and col0 % tn == 0 and s % tm == 0
    per_tile = tn // LANES
    by_kinds = {}
    for t in range(n // tn):
        by_kinds.setdefault(tuple(kinds_per_chunk[t * per_tile:(t + 1) * per_tile]), []).append(t)
    plans = tuple((k, tuple(v)) for k, v in by_kinds.items())
    j0 = col0 // tn
    grid = (n // tn, s // tm)
    side_in, side_out, side_shapes, side_counts = _side_cast_specs(side, grid[0] * grid[1], grid[1])
    return pl.pallas_call(
        functools.partial(_inproj_kernel, plans=plans, side_counts=side_counts),
        grid=grid,
        in_specs=[pl.BlockSpec((tm, d), lambda j, i: (i, 0)),
                  pl.BlockSpec((d, tn), lambda j, i: (0, j + j0)),
                  pl.BlockSpec((tm, LANES), lambda j, i: (i, 0)),
                  pl.BlockSpec((tm, LANES), lambda j, i: (i, 0)),
                  pl.BlockSpec((1, LANES), lambda j, i: (0, 0)),
                  pl.BlockSpec((1, LANES), lambda j, i: (0, 0))] + side_in,
        out_specs=[pl.BlockSpec((tm, tn), lambda j, i: (i, j))] + side_out,
        out_shape=[jax.ShapeDtypeStruct((s, n), BF16)] + side_shapes,
        compiler_params=_params(2),
        name="inproj",
    )(h, w, cos_t, sin_t, gq.reshape(1, LANES), gk.reshape(1, LANES), *[a for a, _ in side])


def _gmlp_kernel(z_ref, lng_ref, lnb_ref, ws_ref, bs_ref, go_ref, o_ref, oa_ref, *, chunk, groups):
    aw = groups * LANES
    for cidx in range(z_ref.shape[0] // chunk):
        rows = slice(cidx * chunk, (cidx + 1) * chunk)
        v = z_ref[rows, aw:].astype(F32)
        mu = jnp.mean(v, axis=-1, keepdims=True)
        vc = v - mu
        var = jnp.mean(vc * vc, axis=-1, keepdims=True)
        vn = (vc * lax.rsqrt(var + EPS) * lng_ref[...] + lnb_ref[...]).astype(BF16)
        for g in range(groups):
            cols = slice(g * LANES, (g + 1) * LANES)
            mixed = jnp.dot(ws_ref[g], vn[:, cols], preferred_element_type=F32) + bs_ref[g]
            oa_ref[:, cols] = z_ref[rows, cols].astype(F32) * mixed
        o_ref[rows, :] = _rms(oa_ref[...], go_ref[...]).astype(o_ref.dtype)


def _gmlp(z, n_cols_a, ln_g, ln_b, w_s, b_s, g_out):
    s = z.shape[0]
    groups, chunk, _ = w_s.shape
    aw = n_cols_a // 2
    assert aw == groups * LANES
    rows = chunk * min(GMLP_CHUNKS_PER_STEP, s // chunk)
    bs_rep = jnp.broadcast_to(b_s[:, :, None], (groups, chunk, LANES))
    return pl.pallas_call(
        functools.partial(_gmlp_kernel, chunk=chunk, groups=groups),
        grid=(s // rows,),
        in_specs=[pl.BlockSpec((rows, 2 * aw), lambda i: (i, 0)),
                  pl.BlockSpec((1, aw), lambda i: (0, 0)),
                  pl.BlockSpec((1, aw), lambda i: (0, 0)),
                  pl.BlockSpec((groups, chunk, chunk), lambda i: (0, 0, 0)),
                  pl.BlockSpec((groups, chunk, LANES), lambda i: (0, 0, 0)),
                  pl.BlockSpec((1, aw), lambda i: (0, 0))],
        out_specs=pl.BlockSpec((rows, aw), lambda i: (i, 0)),
        out_shape=jax.ShapeDtypeStruct((s, aw), BF16),
        scratch_shapes=[pltpu.VMEM((chunk, aw), F32)],
        compiler_params=_params(1),
        name="gmlp",
    )(z, ln_g.reshape(1, aw), ln_b.reshape(1, aw), w_s.astype(BF16), bs_rep, g_out.reshape(1, aw))


def _attn_kernel(q_ref, kc_ref, vc_ref, k_ref, v_ref, go_ref, o_ref, ob_ref, *, group, tk):
    kh = pl.program_id(1)
    n_kh = pl.num_programs(1)

    chunks = [(kc_ref, 0, kc_ref.shape[0])]
    chunks += [(k_ref, c * tk, tk) for c in range(k_ref.shape[0] // tk)]
    v_refs = {id(kc_ref): vc_ref, id(k_ref): v_ref}
    tiles = [(ci, g) for ci in range(len(chunks)) for g in range(group)]

    def scores(tile):
        ci, g = tile
        ref, lo, n = chunks[ci]
        q = q_ref[:, g * HEAD_DIM:(g + 1) * HEAD_DIM]
        return lax.dot_general(ref[lo:lo + n, :], q, (((1,), (1,)), ((), ())),
                               preferred_element_type=F32)

    stats = [None] * group
    accs = [None] * group
    v_t = {}

    def softmax(tile, s):
        g = tile[1]
        m_blk = jnp.max(s, axis=0, keepdims=True)
        if stats[g] is None:
            p = jnp.exp2(s - m_blk)
            stats[g] = (m_blk, jnp.sum(p, axis=0, keepdims=True))
            return tile, p.astype(BF16), None
        m_old, l_old = stats[g]
        m_new = jnp.maximum(m_old, m_blk)
        p = jnp.exp2(s - m_new)
        alpha = jnp.exp2(m_old - m_new)
        stats[g] = (m_new, alpha * l_old + jnp.sum(p, axis=0, keepdims=True))
        return tile, p.astype(BF16), alpha

    def values(pending):
        (ci, g), p, alpha = pending
        if ci not in v_t:
            ref, lo, n = chunks[ci]
            v_t[ci] = v_refs[id(ref)][lo:lo + n, :].T
        pv = jnp.dot(v_t[ci], p, preferred_element_type=F32)
        accs[g] = pv if alpha is None else alpha * accs[g] + pv

    ready = [scores(tiles[t]) for t in range(min(ATTN_SCORE_AHEAD, len(tiles)))]
    pending = []
    for t, tile in enumerate(tiles):
        if t + ATTN_SCORE_AHEAD < len(tiles):
            ready.append(scores(tiles[t + ATTN_SCORE_AHEAD]))
        pending.append(softmax(tile, ready.pop(0)))
        if len(pending) > ATTN_VALUE_LAG:
            values(pending.pop(0))
    for item in pending:
        values(item)

    for g in range(group):
        ob_ref[kh * group + g] = (accs[g] / stats[g][1]).T

    @pl.when(kh == n_kh - 1)
    def _():
        n_heads = ob_ref.shape[0]
        ssq = jnp.zeros((q_ref.shape[0], 1), F32)
        for h in range(n_heads):
            o = ob_ref[h]
            ssq = ssq + jnp.sum(o * o, axis=-1, keepdims=True)
        inv = lax.rsqrt(ssq / (n_heads * HEAD_DIM) + EPS)
        for h in range(n_heads):
            cols = slice(h * HEAD_DIM, (h + 1) * HEAD_DIM)
            o_ref[:, cols] = (ob_ref[h] * inv * go_ref[:, cols]).astype(o_ref.dtype)


def _attention(z, q_col0, k_col0, v_col0, n_heads, n_kv, zc, g_out):
    s = z.shape[0]
    c = zc.shape[0]
    group = n_heads // n_kv
    gw = group * HEAD_DIM
    bw = n_heads * HEAD_DIM
    tq = min(ATTN_TQ, s)
    tk = min(ATTN_TK, s)
    assert q_col0 % gw == 0 and k_col0 % HEAD_DIM == 0 and v_col0 % HEAD_DIM == 0
    qb, kb, vb = q_col0 // gw, k_col0 // HEAD_DIM, v_col0 // HEAD_DIM
    return pl.pallas_call(
        functools.partial(_attn_kernel, group=group, tk=tk),
        grid=(s // tq, n_kv),
        in_specs=[pl.BlockSpec((tq, gw), lambda i, kh: (i, qb + kh)),
                  pl.BlockSpec((c, HEAD_DIM), lambda i, kh: (0, kh)),
                  pl.BlockSpec((c, HEAD_DIM), lambda i, kh: (0, n_kv + kh)),
                  pl.BlockSpec((s, HEAD_DIM), lambda i, kh: (0, kb + kh)),
                  pl.BlockSpec((s, HEAD_DIM), lambda i, kh: (0, vb + kh)),
                  pl.BlockSpec((1, bw), lambda i, kh: (0, 0))],
        out_specs=pl.BlockSpec((tq, bw), lambda i, kh: (i, 0)),
        out_shape=jax.ShapeDtypeStruct((s, bw), BF16),
        scratch_shapes=[pltpu.VMEM((n_heads, tq, HEAD_DIM), F32)],
        compiler_params=_params(2),
        name="attn",
    )(z, zc, zc, z, z, g_out.reshape(1, bw))


def _outproj_kernel(a_ref, b_ref, wa_ref, wb_ref, *refs, side_counts):
    n_side = len(side_counts)
    o_ref = refs[n_side]
    _side_cast(refs[:n_side], refs[n_side + 1:], side_counts)
    o_ref[...] = (jnp.dot(a_ref[...], wa_ref[...], preferred_element_type=F32)
                  + jnp.dot(b_ref[...], wb_ref[...], preferred_element_type=F32))


def _outproj(a, b, w, side=()):
    s, aw = a.shape
    bw = b.shape[1]
    n = w.shape[1]
    tm = min(OUTPROJ_TM, s)
    tn = min(OUTPROJ_TN, n)
    assert aw == bw
    grid = (n // tn, s // tm)
    side_in, side_out, side_shapes, side_counts = _side_cast_specs(side, grid[0] * grid[1], grid[1])
    return pl.pallas_call(
        functools.partial(_outproj_kernel, side_counts=side_counts),
        grid=grid,
        in_specs=[pl.BlockSpec((tm, aw), lambda j, i: (i, 0)),
                  pl.BlockSpec((tm, bw), lambda j, i: (i, 0)),
                  pl.BlockSpec((aw, tn), lambda j, i: (0, j)),
                  pl.BlockSpec((bw, tn), lambda j, i: (1, j))] + side_in,
        out_specs=[pl.BlockSpec((tm, tn), lambda j, i: (i, j))] + side_out,
        out_shape=[jax.ShapeDtypeStruct((s, n), F32)] + side_shapes,
        compiler_params=_params(2),
        name="outproj",
    )(a, b, w, w, *[arr for arr, _ in side])


def _post_kernel(x_ref, o_ref, gpost_ref, gate_ref, gpre_ref, sh_ref, sc_ref, xm_ref, h_ref):
    xm = x_ref[...] + gate_ref[0:1, :] * _rms(o_ref[...], gpost_ref[...])
    xm_ref[...] = xm
    h = _rms(xm, gpre_ref[...])
    h_ref[...] = (h * (1.0 + sc_ref[0:1, :]) + sh_ref[0:1, :]).astype(h_ref.dtype)


def _post(x, o, g_post, mods, g_pre):
    s, d = x.shape
    tm = min(ROW_TILE, s)
    row = pl.BlockSpec((tm, d), lambda i: (i, 0))
    vec = pl.BlockSpec((1, d), lambda i: (0, 0))

    def mod(idx):
        return pl.BlockSpec((2, d), lambda i: (0, idx))

    return pl.pallas_call(
        _post_kernel,
        grid=(s // tm,),
        in_specs=[row, row, vec, mod(2), vec, mod(3), mod(4)],
        out_specs=[row, row],
        out_shape=[jax.ShapeDtypeStruct((s, d), F32), jax.ShapeDtypeStruct((s, d), BF16)],
        compiler_params=_params(1),
        name="post",
    )(x, o, g_post.reshape(1, d), mods, g_pre.reshape(1, d), mods, mods)


def _ffn_kernel(hp_ref, hm_ref, hn_ref, w_ref, cw_ref, cb_ref, wd_ref,
                o_ref, lhs_ref, r0_ref, r1_ref, act_ref, *, nf):
    i = pl.program_id(0)
    f = pl.program_id(1)
    tm = hm_ref.shape[0]
    halo = hp_ref.shape[0]
    tf = act_ref.shape[1]
    r_refs = (r0_ref, r1_ref)

    def up_proj(r_ref):
        r_ref[...] = jnp.dot(lhs_ref[...], w_ref[...], preferred_element_type=F32)

    def conv(win, cols):
        n = win.shape[0]
        prev = pltpu.roll(win, 1, 0)[SUBLANES:n - SUBLANES, :]
        nxt = pltpu.roll(win, n - 1, 0)[SUBLANES:n - SUBLANES, :]
        return (prev * cw_ref[0:1, cols] + win[SUBLANES:n - SUBLANES, :] * cw_ref[1:2, cols]
                + nxt * cw_ref[2:3, cols] + cb_ref[:, cols])

    def activation(r_ref):
        rb = min(FFN_ACT_ROWS, tm)
        g_cols, u_cols = slice(0, tf), slice(tf, 2 * tf)
        for b in range(tm // rb):
            lo = halo + b * rb - SUBLANES
            gate = conv(r_ref[lo:lo + rb + 2 * SUBLANES, g_cols], g_cols)
            up = conv(r_ref[lo:lo + rb + 2 * SUBLANES, u_cols], u_cols)
            act_ref[b * rb:(b + 1) * rb, :] = (gate * jax.nn.sigmoid(gate) * up).astype(BF16)

    def down_proj():
        o_ref[...] += jnp.dot(act_ref[...], wd_ref[...], preferred_element_type=F32)

    @pl.when(f == 0)
    def _():
        lhs_ref[0:halo, :] = jnp.where(i > 0, hp_ref[...], jnp.zeros_like(hp_ref[...]))
        lhs_ref[halo:halo + tm, :] = hm_ref[...]
        lhs_ref[halo + tm:, :] = jnp.where(i < pl.num_programs(0) - 1, hn_ref[...],
                                           jnp.zeros_like(hn_ref[...]))
        o_ref[...] = jnp.zeros(o_ref.shape, F32)
        up_proj(r0_ref)

    for parity in (0, 1):
        @pl.when(jnp.logical_and(jnp.logical_and(f > 0, f < nf), f % 2 == parity))
        def _(parity=parity):
            activation(r_refs[1 - parity])
            up_proj(r_refs[parity])
            down_proj()

    @pl.when(f == nf)
    def _():
        activation(r_refs[(nf - 1) % 2])
        down_proj()


def _pair_tiles(a, tf):
    lead, cols = a.shape
    return a.reshape(lead, 2, cols // (2 * tf), tf).transpose(0, 2, 1, 3).reshape(lead, cols)


def _ffn(h, w_gu, conv_w, conv_b, w_down):
    s, d = h.shape
    ff = w_down.shape[0]
    tm = min(FFN_TM, s)
    tf = min(FFN_TF, ff)
    assert ff % tf == 0 and s % tm == 0 and tm % BF16_ROWS == 0
    nf = ff // tf
    conv_w, conv_b = _pair_tiles(conv_w, tf), _pair_tiles(conv_b, tf)
    hb = tm // BF16_ROWS
    n_hb = s // BF16_ROWS
    single = pl.Buffered(1)

    def cur(f):
        return jnp.minimum(f, nf - 1)

    def prv(f):
        return jnp.maximum(f - 1, 0)

    return pl.pallas_call(
        functools.partial(_ffn_kernel, nf=nf),
        grid=(s // tm, nf + 1),
        in_specs=[pl.BlockSpec((BF16_ROWS, d), lambda i, f: (jnp.maximum(i * hb - 1, 0), 0)),
                  pl.BlockSpec((tm, d), lambda i, f: (i, 0), pipeline_mode=single),
                  pl.BlockSpec((BF16_ROWS, d), lambda i, f: (jnp.minimum((i + 1) * hb, n_hb - 1), 0)),
                  pl.BlockSpec((d, 2 * tf), lambda i, f: (0, cur(f))),
                  pl.BlockSpec((3, 2 * tf), lambda i, f: (0, prv(f))),
                  pl.BlockSpec((1, 2 * tf), lambda i, f: (0, prv(f))),
                  pl.BlockSpec((tf, d), lambda i, f: (prv(f), 0))],
        out_specs=pl.BlockSpec((tm, d), lambda i, f: (i, 0), pipeline_mode=single),
        out_shape=jax.ShapeDtypeStruct((s, d), F32),
        scratch_shapes=[pltpu.VMEM((tm + 2 * BF16_ROWS, d), BF16),
                        pltpu.VMEM((tm + 2 * BF16_ROWS, 2 * tf), F32),
                        pltpu.VMEM((tm + 2 * BF16_ROWS, 2 * tf), F32),
                        pltpu.VMEM((tm, tf), BF16)],
        compiler_params=_params(2),
        name="ffn",
    )(h, h, h, w_gu, conv_w, conv_b, w_down)


def _final_kernel(xm_ref, f_ref, g_ref, gate_ref, o_ref):
    o_ref[...] = xm_ref[...] + gate_ref[0:1, :] * _rms(f_ref[...], g_ref[...])


def _final(xm, f, g_post, mods):
    s, d = xm.shape
    tm = min(ROW_TILE, s)
    row = pl.BlockSpec((tm, d), lambda i: (i, 0))
    return pl.pallas_call(
        _final_kernel,
        grid=(s // tm,),
        in_specs=[row, row, pl.BlockSpec((1, d), lambda i: (0, 0)),
                  pl.BlockSpec((2, d), lambda i: (0, 5))],
        out_specs=row,
        out_shape=jax.ShapeDtypeStruct((s, d), F32),
        compiler_params=_params(1),
        name="final",
    )(xm, f, g_post.reshape(1, d), mods)


def _rope_tables(s):
    quarter = HEAD_DIM // 4
    pos = np.arange(s)
    freqs = ROPE_THETA ** (-np.arange(quarter, dtype=np.float64) / quarter)
    ang_r = (pos // GRID_W)[:, None] * freqs[None, :]
    ang_c = (pos % GRID_W)[:, None] * freqs[None, :]
    cos_t = np.concatenate([np.cos(ang_r), np.cos(ang_r), np.cos(ang_c), np.cos(ang_c)], axis=1)
    sin_t = np.concatenate([-np.sin(ang_r), np.sin(ang_r), -np.sin(ang_c), np.sin(ang_c)], axis=1)
    return jnp.asarray(cos_t, F32), jnp.asarray(sin_t, F32)


def kernel(x, c, ctx, c_ctx, w_ada, b_ada, g_pre_mix, g_post_mix, g_pre_ffn, g_post_ffn, w_in, ln_v_g,
           ln_v_b, w_s, b_s, g_q, g_k, g_out_a, g_out_b, w_out, w_up, conv_w, conv_b, w_down):
    bsz, s, d = x.shape
    assert bsz == 1 and w_in.shape[0] == 1, "single batch element, single layer"
    aw = ln_v_g.shape[-1]
    bw = g_out_b.shape[-1]
    in_cols = w_in.shape[-1]
    kvw = (in_cols - 2 * aw - bw) // 2
    n_heads, n_kv = bw // HEAD_DIM, kvw // HEAD_DIM
    q0, k0, v0 = 2 * aw, 2 * aw + bw, 2 * aw + bw + kvw
    ff = w_down.shape[1]

    x2 = x[0]
    ctx2 = ctx[0]
    w_in_b = w_in[0].astype(BF16)

    mods = _adaln(jnp.concatenate([c, c_ctx[None, :]], axis=0), w_ada[0], b_ada[0])

    cos_t, sin_t = _rope_tables(s)
    h = _prenorm(x2, g_pre_mix[0], mods, 0, 0, 1)
    hc = _prenorm(ctx2, g_pre_mix[0], mods, 1, 0, 1)

    kinds = (["gelu"] * (2 * aw // LANES) + ["q"] * n_heads + ["k"] * n_kv + ["v"] * n_kv)
    z, w_gu_b, w_out_b = _inproj(h, w_in_b, 0, kinds, cos_t, sin_t, g_q[0], g_k[0],
                                 side=((w_up[0], min(FFN_TF, ff)), (w_out[0], 0)))
    zc, = _inproj(hc, w_in_b, k0, ["kc"] * n_kv + ["v"] * n_kv, cos_t, sin_t, g_q[0], g_k[0])

    out_a = _gmlp(z, 2 * aw, ln_v_g[0], ln_v_b[0], w_s[0], b_s[0], g_out_a[0])
    out_b = _attention(z, q0, k0, v0, n_heads, n_kv, zc, g_out_b[0])
    o, w_down_b = _outproj(out_a, out_b, w_out_b, side=((w_down[0], 0),))
    x_mix, h2 = _post(x2, o, g_post_mix[0], mods, g_pre_ffn[0])
    f = _ffn(h2, w_gu_b, conv_w[0], conv_b[0].reshape(1, 2 * ff), w_down_b)
    out = _final(x_mix, f, g_post_ffn[0], mods)
    return out[None]
```

```python
import functools

import numpy as np
import jax
import jax.numpy as jnp
from jax import lax
from jax.experimental import pallas as pl
from jax.experimental.pallas import tpu as pltpu

GRID_W = 64
HEAD_DIM = 128
ROPE_THETA = 10000.0
EPS = 1e-6
LOG2_E = 1.4426950408889634

LANES = 128
SUBLANES = 8
BF16_ROWS = 16
VMEM_LIMIT_BYTES = 56 * 1024 * 1024

ADALN_TN = 1024
ROW_TILE = 256
INPROJ_TM = 512
INPROJ_TN = 1024
GMLP_CHUNKS_PER_STEP = 4
ATTN_TQ = 256
ATTN_TK = 1024
ATTN_SCORE_AHEAD = 2
ATTN_VALUE_LAG = 2
OUTPROJ_TM = 512
OUTPROJ_TN = 1024
FFN_TM = 1024
FFN_TF = 256
FFN_ACT_ROWS = 256

F32 = jnp.float32
BF16 = jnp.bfloat16


def _params(n_axes):
    return pltpu.CompilerParams(dimension_semantics=("arbitrary",) * n_axes,
                                vmem_limit_bytes=VMEM_LIMIT_BYTES)


def _rms(x, g):
    return x * lax.rsqrt(jnp.mean(x * x, axis=-1, keepdims=True) + EPS) * g


def _adaln_kernel(c_ref, w_ref, b_ref, o_ref, s_ref):
    k_dim, tn = w_ref.shape
    nj = tn // LANES

    @pl.when(pl.program_id(0) == 0)
    def _():
        c = c_ref[...]
        s_ref[...] = c * jax.nn.sigmoid(c)

    def body(r, accs):
        rows = pl.ds(pl.multiple_of(r * SUBLANES, SUBLANES), SUBLANES)
        s0 = s_ref[0, rows, :]
        s1 = s_ref[1, rows, :]
        new = []
        for j in range(nj):
            w = w_ref[rows, j * LANES:(j + 1) * LANES]
            new.append(accs[2 * j] + w * s0)
            new.append(accs[2 * j + 1] + w * s1)
        return tuple(new)

    zero = jnp.zeros((SUBLANES, LANES), F32)
    accs = lax.fori_loop(0, k_dim // SUBLANES, body, (zero,) * (2 * nj), unroll=4)
    for j in range(nj):
        cols = slice(j * LANES, (j + 1) * LANES)
        o_ref[0:1, cols] = jnp.sum(accs[2 * j], axis=0, keepdims=True) + b_ref[:, cols]
        o_ref[1:2, cols] = jnp.sum(accs[2 * j + 1], axis=0, keepdims=True) + b_ref[:, cols]


def _adaln(cc, w_ada, b_ada):
    k_dim, n = w_ada.shape
    tn = min(ADALN_TN, n)
    c_rep = jnp.broadcast_to(cc[:, :, None], (2, k_dim, LANES))
    return pl.pallas_call(
        _adaln_kernel,
        grid=(n // tn,),
        in_specs=[pl.BlockSpec((2, k_dim, LANES), lambda j: (0, 0, 0)),
                  pl.BlockSpec((k_dim, tn), lambda j: (0, j)),
                  pl.BlockSpec((1, tn), lambda j: (0, j))],
        out_specs=pl.BlockSpec((2, tn), lambda j: (0, j)),
        out_shape=jax.ShapeDtypeStruct((2, n), F32),
        scratch_shapes=[pltpu.VMEM((2, k_dim, LANES), F32)],
        compiler_params=_params(1),
        name="adaln",
    )(c_rep, w_ada, b_ada.reshape(1, n))


def _prenorm_kernel(x_ref, g_ref, sh_ref, sc_ref, o_ref, *, row):
    y = _rms(x_ref[...], g_ref[...])
    o_ref[...] = (y * (1.0 + sc_ref[row:row + 1, :]) + sh_ref[row:row + 1, :]).astype(o_ref.dtype)


def _prenorm(x, g, mods, row, shift_idx, scale_idx):
    s, d = x.shape
    tm = min(ROW_TILE, s)
    return pl.pallas_call(
        functools.partial(_prenorm_kernel, row=row),
        grid=(s // tm,),
        in_specs=[pl.BlockSpec((tm, d), lambda i: (i, 0)),
                  pl.BlockSpec((1, d), lambda i: (0, 0)),
                  pl.BlockSpec((2, d), lambda i: (0, shift_idx)),
                  pl.BlockSpec((2, d), lambda i: (0, scale_idx))],
        out_specs=pl.BlockSpec((tm, d), lambda i: (i, 0)),
        out_shape=jax.ShapeDtypeStruct((s, d), BF16),
        compiler_params=_params(1),
        name="prenorm",
    )(x, g.reshape(1, d), mods, mods)


def _swap_rope_halves(a):
    lane = lax.broadcasted_iota(jnp.int32, a.shape, 1)
    return jnp.where((lane & 32) != 0, pltpu.roll(a, 32, 1), pltpu.roll(a, LANES - 32, 1))


def _head_epilogue(a, kind, cos, sin, gq, gk):
    if kind == "gelu":
        return 0.5 * a * (1.0 + lax.erf(a * (2.0 ** -0.5)))
    if kind == "v":
        return a
    a = _rms(a, gq if kind == "q" else gk)
    if kind != "kc":
        a = a * cos + _swap_rope_halves(a) * sin
    if kind == "q":
        a = a * (HEAD_DIM ** -0.5 * LOG2_E)
    return a


def _side_cast_specs(arrays, n_steps, n_inner):
    in_specs, out_specs, out_shapes, counts = [], [], [], []
    for a in arrays:
        rows, cols = a.shape
        rb = next(r for r in range(BF16_ROWS, rows + 1, BF16_ROWS)
                  if rows % r == 0 and rows // r <= n_steps)
        nb = rows // rb

        def index(j, i, nb=nb):
            return (jnp.minimum(j * n_inner + i, nb - 1), 0)

        in_specs.append(pl.BlockSpec((rb, cols), index))
        out_specs.append(pl.BlockSpec((rb, cols), index))
        out_shapes.append(jax.ShapeDtypeStruct((rows, cols), BF16))
        counts.append(nb)
    return in_specs, out_specs, out_shapes, tuple(counts)


def _side_cast(src_refs, dst_refs, counts):
    step = pl.program_id(0) * pl.num_programs(1) + pl.program_id(1)
    for src, dst, nb in zip(src_refs, dst_refs, counts):
        @pl.when(step < nb)
        def _(src=src, dst=dst):
            dst[...] = src[...].astype(dst.dtype)


def _inproj_kernel(h_ref, w_ref, cos_ref, sin_ref, gq_ref, gk_ref, *refs, plans, side_counts):
    n_side = len(side_counts)
    o_ref = refs[n_side]
    _side_cast(refs[:n_side], refs[n_side + 1:], side_counts)
    j = pl.program_id(0)
    for kinds, tiles in plans:
        cond = functools.reduce(jnp.logical_or, [j == t for t in tiles])

        @pl.when(cond)
        def _(kinds=kinds):
            acc = jnp.dot(h_ref[...], w_ref[...], preferred_element_type=F32)
            for c, kind in enumerate(kinds):
                cols = slice(c * LANES, (c + 1) * LANES)
                o_ref[:, cols] = _head_epilogue(acc[:, cols], kind, cos_ref[...], sin_ref[...],
                                                gq_ref[...], gk_ref[...]).astype(o_ref.dtype)


def _inproj(h, w, col0, kinds_per_chunk, cos_t, sin_t, gq, gk, side=()):
    s, d = h.shape
    n = LANES * len(kinds_per_chunk)
    tm = min(INPROJ_TM, s)
    tn = min(INPROJ_TN, n)
    assert n % tn == 0 and col0 % tn == 0 and s % tm == 0
    per_tile = tn // LANES
    by_kinds = {}
    for t in range(n // tn):
        by_kinds.setdefault(tuple(kinds_per_chunk[t * per_tile:(t + 1) * per_tile]), []).append(t)
    plans = tuple((k, tuple(v)) for k, v in by_kinds.items())
    j0 = col0 // tn
    grid = (n // tn, s // tm)
    side_in, side_out, side_shapes, side_counts = _side_cast_specs(side, grid[0] * grid[1], grid[1])
    return pl.pallas_call(
        functools.partial(_inproj_kernel, plans=plans, side_counts=side_counts),
        grid=grid,
        in_specs=[pl.BlockSpec((tm, d), lambda j, i: (i, 0)),
                  pl.BlockSpec((d, tn), lambda j, i: (0, j + j0)),
                  pl.BlockSpec((tm, LANES), lambda j, i: (i, 0)),
                  pl.BlockSpec((tm, LANES), lambda j, i: (i, 0)),
                  pl.BlockSpec((1, LANES), lambda j, i: (0, 0)),
                  pl.BlockSpec((1, LANES), lambda j, i: (0, 0))] + side_in,
        out_specs=[pl.BlockSpec((tm, tn), lambda j, i: (i, j))] + side_out,
        out_shape=[jax.ShapeDtypeStruct((s, n), BF16)] + side_shapes,
        compiler_params=_params(2),
        name="inproj",
    )(h, w, cos_t, sin_t, gq.reshape(1, LANES), gk.reshape(1, LANES), *side)


def _gmlp_kernel(z_ref, lng_ref, lnb_ref, ws_ref, bs_ref, go_ref, o_ref, oa_ref, *, chunk, groups):
    aw = groups * LANES
    for cidx in range(z_ref.shape[0] // chunk):
        rows = slice(cidx * chunk, (cidx + 1) * chunk)
        v = z_ref[rows, aw:].astype(F32)
        mu = jnp.mean(v, axis=-1, keepdims=True)
        vc = v - mu
        var = jnp.mean(vc * vc, axis=-1, keepdims=True)
        vn = (vc * lax.rsqrt(var + EPS) * lng_ref[...] + lnb_ref[...]).astype(BF16)
        for g in range(groups):
            cols = slice(g * LANES, (g + 1) * LANES)
            mixed = jnp.dot(ws_ref[g], vn[:, cols], preferred_element_type=F32) + bs_ref[g]
            oa_ref[:, cols] = z_ref[rows, cols].astype(F32) * mixed
        o_ref[rows, :] = _rms(oa_ref[...], go_ref[...]).astype(o_ref.dtype)


def _gmlp(z, n_cols_a, ln_g, ln_b, w_s, b_s, g_out):
    s = z.shape[0]
    groups, chunk, _ = w_s.shape
    aw = n_cols_a // 2
    assert aw == groups * LANES
    rows = chunk * min(GMLP_CHUNKS_PER_STEP, s // chunk)
    bs_rep = jnp.broadcast_to(b_s[:, :, None], (groups, chunk, LANES))
    return pl.pallas_call(
        functools.partial(_gmlp_kernel, chunk=chunk, groups=groups),
        grid=(s // rows,),
        in_specs=[pl.BlockSpec((rows, 2 * aw), lambda i: (i, 0)),
                  pl.BlockSpec((1, aw), lambda i: (0, 0)),
                  pl.BlockSpec((1, aw), lambda i: (0, 0)),
                  pl.BlockSpec((groups, chunk, chunk), lambda i: (0, 0, 0)),
                  pl.BlockSpec((groups, chunk, LANES), lambda i: (0, 0, 0)),
                  pl.BlockSpec((1, aw), lambda i: (0, 0))],
        out_specs=pl.BlockSpec((rows, aw), lambda i: (i, 0)),
        out_shape=jax.ShapeDtypeStruct((s, aw), BF16),
        scratch_shapes=[pltpu.VMEM((chunk, aw), F32)],
        compiler_params=_params(1),
        name="gmlp",
    )(z, ln_g.reshape(1, aw), ln_b.reshape(1, aw), w_s.astype(BF16), bs_rep, g_out.reshape(1, aw))


def _attn_kernel(q_ref, kc_ref, vc_ref, k_ref, v_ref, go_ref, o_ref, ob_ref, *, group, tk):
    kh = pl.program_id(1)
    n_kh = pl.num_programs(1)

    chunks = [(kc_ref, 0, kc_ref.shape[0])]
    chunks += [(k_ref, c * tk, tk) for c in range(k_ref.shape[0] // tk)]
    v_refs = {id(kc_ref): vc_ref, id(k_ref): v_ref}
    tiles = [(ci, g) for ci in range(len(chunks)) for g in range(group)]

    def scores(tile):
        ci, g = tile
        ref, lo, n = chunks[ci]
        q = q_ref[:, g * HEAD_DIM:(g + 1) * HEAD_DIM]
        return lax.dot_general(ref[lo:lo + n, :], q, (((1,), (1,)), ((), ())),
                               preferred_element_type=F32)

    stats = [None] * group
    accs = [None] * group
    v_t = {}

    def softmax(tile, s):
        g = tile[1]
        m_blk = jnp.max(s, axis=0, keepdims=True)
        if stats[g] is None:
            p = jnp.exp2(s - m_blk)
            stats[g] = (m_blk, jnp.sum(p, axis=0, keepdims=True))
            return tile, p.astype(BF16), None
        m_old, l_old = stats[g]
        m_new = jnp.maximum(m_old, m_blk)
        p = jnp.exp2(s - m_new)
        alpha = jnp.exp2(m_old - m_new)
        stats[g] = (m_new, alpha * l_old + jnp.sum(p, axis=0, keepdims=True))
        return tile, p.astype(BF16), alpha

    def values(pending):
        (ci, g), p, alpha = pending
        if ci not in v_t:
            ref, lo, n = chunks[ci]
            v_t[ci] = v_refs[id(ref)][lo:lo + n, :].T
        pv = jnp.dot(v_t[ci], p, preferred_element_type=F32)
        accs[g] = pv if alpha is None else alpha * accs[g] + pv

    ready = [scores(tiles[t]) for t in range(min(ATTN_SCORE_AHEAD, len(tiles)))]
    pending = []
    for t, tile in enumerate(tiles):
        if t + ATTN_SCORE_AHEAD < len(tiles):
            ready.append(scores(tiles[t + ATTN_SCORE_AHEAD]))
        pending.append(softmax(tile, ready.pop(0)))
        if len(pending) > ATTN_VALUE_LAG:
            values(pending.pop(0))
    for item in pending:
        values(item)

    for g in range(group):
        ob_ref[kh * group + g] = (accs[g] / stats[g][1]).T

    @pl.when(kh == n_kh - 1)
    def _():
        n_heads = ob_ref.shape[0]
        ssq = jnp.zeros((q_ref.shape[0], 1), F32)
        for h in range(n_heads):
            o = ob_ref[h]
            ssq = ssq + jnp.sum(o * o, axis=-1, keepdims=True)
        inv = lax.rsqrt(ssq / (n_heads * HEAD_DIM) + EPS)
        for h in range(n_heads):
            cols = slice(h * HEAD_DIM, (h + 1) * HEAD_DIM)
            o_ref[:, cols] = (ob_ref[h] * inv * go_ref[:, cols]).astype(o_ref.dtype)


def _attention(z, q_col0, k_col0, v_col0, n_heads, n_kv, zc, g_out):
    s = z.shape[0]
    c = zc.shape[0]
    group = n_heads // n_kv
    gw = group * HEAD_DIM
    bw = n_heads * HEAD_DIM
    tq = min(ATTN_TQ, s)
    tk = min(ATTN_TK, s)
    assert q_col0 % gw == 0 and k_col0 % HEAD_DIM == 0 and v_col0 % HEAD_DIM == 0
    qb, kb, vb = q_col0 // gw, k_col0 // HEAD_DIM, v_col0 // HEAD_DIM
    return pl.pallas_call(
        functools.partial(_attn_kernel, group=group, tk=tk),
        grid=(s // tq, n_kv),
        in_specs=[pl.BlockSpec((tq, gw), lambda i, kh: (i, qb + kh)),
                  pl.BlockSpec((c, HEAD_DIM), lambda i, kh: (0, kh)),
                  pl.BlockSpec((c, HEAD_DIM), lambda i, kh: (0, n_kv + kh)),
                  pl.BlockSpec((s, HEAD_DIM), lambda i, kh: (0, kb + kh)),
                  pl.BlockSpec((s, HEAD_DIM), lambda i, kh: (0, vb + kh)),
                  pl.BlockSpec((1, bw), lambda i, kh: (0, 0))],
        out_specs=pl.BlockSpec((tq, bw), lambda i, kh: (i, 0)),
        out_shape=jax.ShapeDtypeStruct((s, bw), BF16),
        scratch_shapes=[pltpu.VMEM((n_heads, tq, HEAD_DIM), F32)],
        compiler_params=_params(2),
        name="attn",
    )(z, zc, zc, z, z, g_out.reshape(1, bw))


def _outproj_kernel(a_ref, b_ref, wa_ref, wb_ref, *refs, side_counts):
    n_side = len(side_counts)
    o_ref = refs[n_side]
    _side_cast(refs[:n_side], refs[n_side + 1:], side_counts)
    o_ref[...] = (jnp.dot(a_ref[...], wa_ref[...], preferred_element_type=F32)
                  + jnp.dot(b_ref[...], wb_ref[...], preferred_element_type=F32))


def _outproj(a, b, w, side=()):
    s, aw = a.shape
    bw = b.shape[1]
    n = w.shape[1]
    tm = min(OUTPROJ_TM, s)
    tn = min(OUTPROJ_TN, n)
    assert aw == bw
    grid = (n // tn, s // tm)
    side_in, side_out, side_shapes, side_counts = _side_cast_specs(side, grid[0] * grid[1], grid[1])
    return pl.pallas_call(
        functools.partial(_outproj_kernel, side_counts=side_counts),
        grid=grid,
        in_specs=[pl.BlockSpec((tm, aw), lambda j, i: (i, 0)),
                  pl.BlockSpec((tm, bw), lambda j, i: (i, 0)),
                  pl.BlockSpec((aw, tn), lambda j, i: (0, j)),
                  pl.BlockSpec((bw, tn), lambda j, i: (1, j))] + side_in,
        out_specs=[pl.BlockSpec((tm, tn), lambda j, i: (i, j))] + side_out,
        out_shape=[jax.ShapeDtypeStruct((s, n), F32)] + side_shapes,
        compiler_params=_params(2),
        name="outproj",
    )(a, b, w, w, *side)


def _post_kernel(x_ref, o_ref, gpost_ref, gate_ref, gpre_ref, sh_ref, sc_ref, xm_ref, h_ref):
    xm = x_ref[...] + gate_ref[0:1, :] * _rms(o_ref[...], gpost_ref[...])
    xm_ref[...] = xm
    h = _rms(xm, gpre_ref[...])
    h_ref[...] = (h * (1.0 + sc_ref[0:1, :]) + sh_ref[0:1, :]).astype(h_ref.dtype)


def _post(x, o, g_post, mods, g_pre):
    s, d = x.shape
    tm = min(ROW_TILE, s)
    row = pl.BlockSpec((tm, d), lambda i: (i, 0))
    vec = pl.BlockSpec((1, d), lambda i: (0, 0))

    def mod(idx):
        return pl.BlockSpec((2, d), lambda i: (0, idx))

    return pl.pallas_call(
        _post_kernel,
        grid=(s // tm,),
        in_specs=[row, row, vec, mod(2), vec, mod(3), mod(4)],
        out_specs=[row, row],
        out_shape=[jax.ShapeDtypeStruct((s, d), F32), jax.ShapeDtypeStruct((s, d), BF16)],
        compiler_params=_params(1),
        name="post",
    )(x, o, g_post.reshape(1, d), mods, g_pre.reshape(1, d), mods, mods)


def _ffn_kernel(hp_ref, hm_ref, hn_ref, wg_ref, wu_ref, cwg_ref, cwu_ref, cbg_ref, cbu_ref, wd_ref,
                o_ref, lhs_ref, r0_ref, r1_ref, act_ref, *, nf):
    i = pl.program_id(0)
    f = pl.program_id(1)
    tm = hm_ref.shape[0]
    halo = hp_ref.shape[0]
    r_refs = (r0_ref, r1_ref)

    def up_proj(r_ref, which):
        w_ref = (wg_ref, wu_ref)[which]
        r_ref[which] = jnp.dot(lhs_ref[...], w_ref[...], preferred_element_type=F32)

    def conv(win, cw_ref, cb_ref):
        n = win.shape[0]
        prev = pltpu.roll(win, 1, 0)[SUBLANES:n - SUBLANES, :]
        nxt = pltpu.roll(win, n - 1, 0)[SUBLANES:n - SUBLANES, :]
        return (prev * cw_ref[0:1, :] + win[SUBLANES:n - SUBLANES, :] * cw_ref[1:2, :]
                + nxt * cw_ref[2:3, :] + cb_ref[...])

    def activation(r_ref):
        rb = min(FFN_ACT_ROWS, tm)
        for b in range(tm // rb):
            lo = halo + b * rb - SUBLANES
            gate = conv(r_ref[0, lo:lo + rb + 2 * SUBLANES, :], cwg_ref, cbg_ref)
            up = conv(r_ref[1, lo:lo + rb + 2 * SUBLANES, :], cwu_ref, cbu_ref)
            act_ref[b * rb:(b + 1) * rb, :] = (gate * jax.nn.sigmoid(gate) * up).astype(BF16)

    def down_proj():
        o_ref[...] += jnp.dot(act_ref[...], wd_ref[...], preferred_element_type=F32)

    @pl.when(f == 0)
    def _():
        lhs_ref[0:halo, :] = jnp.where(i > 0, hp_ref[...], jnp.zeros_like(hp_ref[...]))
        lhs_ref[halo:halo + tm, :] = hm_ref[...]
        lhs_ref[halo + tm:, :] = jnp.where(i < pl.num_programs(0) - 1, hn_ref[...],
                                           jnp.zeros_like(hn_ref[...]))
        o_ref[...] = jnp.zeros(o_ref.shape, F32)
        up_proj(r0_ref, 0)
        up_proj(r0_ref, 1)

    for parity in (0, 1):
        @pl.when(jnp.logical_and(jnp.logical_and(f > 0, f < nf), f % 2 == parity))
        def _(parity=parity):
            activation(r_refs[1 - parity])
            up_proj(r_refs[parity], 0)
            down_proj()
            up_proj(r_refs[parity], 1)

    @pl.when(f == nf)
    def _():
        activation(r_refs[(nf - 1) % 2])
        down_proj()


def _ffn(h, w_gu, conv_w, conv_b, w_down):
    s, d = h.shape
    ff = w_down.shape[0]
    tm = min(FFN_TM, s)
    tf = min(FFN_TF, ff)
    assert ff % tf == 0 and s % tm == 0 and tm % BF16_ROWS == 0
    nf = ff // tf
    hb = tm // BF16_ROWS
    n_hb = s // BF16_ROWS
    single = pl.Buffered(1)

    def cur(f):
        return jnp.minimum(f, nf - 1)

    def prv(f):
        return jnp.maximum(f - 1, 0)

    return pl.pallas_call(
        functools.partial(_ffn_kernel, nf=nf),
        grid=(s // tm, nf + 1),
        in_specs=[pl.BlockSpec((BF16_ROWS, d), lambda i, f: (jnp.maximum(i * hb - 1, 0), 0)),
                  pl.BlockSpec((tm, d), lambda i, f: (i, 0), pipeline_mode=single),
                  pl.BlockSpec((BF16_ROWS, d), lambda i, f: (jnp.minimum((i + 1) * hb, n_hb - 1), 0)),
                  pl.BlockSpec((d, tf), lambda i, f: (0, cur(f))),
                  pl.BlockSpec((d, tf), lambda i, f: (0, nf + cur(f))),
                  pl.BlockSpec((3, tf), lambda i, f: (0, prv(f))),
                  pl.BlockSpec((3, tf), lambda i, f: (0, nf + prv(f))),
                  pl.BlockSpec((1, tf), lambda i, f: (0, prv(f))),
                  pl.BlockSpec((1, tf), lambda i, f: (0, nf + prv(f))),
                  pl.BlockSpec((tf, d), lambda i, f: (prv(f), 0))],
        out_specs=pl.BlockSpec((tm, d), lambda i, f: (i, 0), pipeline_mode=single),
        out_shape=jax.ShapeDtypeStruct((s, d), F32),
        scratch_shapes=[pltpu.VMEM((tm + 2 * BF16_ROWS, d), BF16),
                        pltpu.VMEM((2, tm + 2 * BF16_ROWS, tf), F32),
                        pltpu.VMEM((2, tm + 2 * BF16_ROWS, tf), F32),
                        pltpu.VMEM((tm, tf), BF16)],
        compiler_params=_params(2),
        name="ffn",
    )(h, h, h, w_gu, w_gu, conv_w, conv_w, conv_b, conv_b, w_down)


def _final_kernel(xm_ref, f_ref, g_ref, gate_ref, o_ref):
    o_ref[...] = xm_ref[...] + gate_ref[0:1, :] * _rms(f_ref[...], g_ref[...])


def _final(xm, f, g_post, mods):
    s, d = xm.shape
    tm = min(ROW_TILE, s)
    row = pl.BlockSpec((tm, d), lambda i: (i, 0))
    return pl.pallas_call(
        _final_kernel,
        grid=(s // tm,),
        in_specs=[row, row, pl.BlockSpec((1, d), lambda i: (0, 0)),
                  pl.BlockSpec((2, d), lambda i: (0, 5))],
        out_specs=row,
        out_shape=jax.ShapeDtypeStruct((s, d), F32),
        compiler_params=_params(1),
        name="final",
    )(xm, f, g_post.reshape(1, d), mods)


def _rope_tables(s):
    quarter = HEAD_DIM // 4
    pos = np.arange(s)
    freqs = ROPE_THETA ** (-np.arange(quarter, dtype=np.float64) / quarter)
    ang_r = (pos // GRID_W)[:, None] * freqs[None, :]
    ang_c = (pos % GRID_W)[:, None] * freqs[None, :]
    cos_t = np.concatenate([np.cos(ang_r), np.cos(ang_r), np.cos(ang_c), np.cos(ang_c)], axis=1)
    sin_t = np.concatenate([-np.sin(ang_r), np.sin(ang_r), -np.sin(ang_c), np.sin(ang_c)], axis=1)
    return jnp.asarray(cos_t, F32), jnp.asarray(sin_t, F32)


def kernel(x, c, ctx, c_ctx, w_ada, b_ada, g_pre_mix, g_post_mix, g_pre_ffn, g_post_ffn, w_in, ln_v_g,
           ln_v_b, w_s, b_s, g_q, g_k, g_out_a, g_out_b, w_out, w_up, conv_w, conv_b, w_down):
    bsz, s, d = x.shape
    assert bsz == 1 and w_in.shape[0] == 1, "single batch element, single layer"
    aw = ln_v_g.shape[-1]
    bw = g_out_b.shape[-1]
    in_cols = w_in.shape[-1]
    kvw = (in_cols - 2 * aw - bw) // 2
    n_heads, n_kv = bw // HEAD_DIM, kvw // HEAD_DIM
    q0, k0, v0 = 2 * aw, 2 * aw + bw, 2 * aw + bw + kvw
    ff = w_down.shape[1]

    x2 = x[0]
    ctx2 = ctx[0]
    w_in_b = w_in[0].astype(BF16)

    mods = _adaln(jnp.concatenate([c, c_ctx[None, :]], axis=0), w_ada[0], b_ada[0])

    cos_t, sin_t = _rope_tables(s)
    h = _prenorm(x2, g_pre_mix[0], mods, 0, 0, 1)
    hc = _prenorm(ctx2, g_pre_mix[0], mods, 1, 0, 1)

    kinds = (["gelu"] * (2 * aw // LANES) + ["q"] * n_heads + ["k"] * n_kv + ["v"] * n_kv)
    z, w_gu_b, w_out_b = _inproj(h, w_in_b, 0, kinds, cos_t, sin_t, g_q[0], g_k[0],
                                 side=(w_up[0], w_out[0]))
    zc, = _inproj(hc, w_in_b, k0, ["kc"] * n_kv + ["v"] * n_kv, cos_t, sin_t, g_q[0], g_k[0])

    out_a = _gmlp(z, 2 * aw, ln_v_g[0], ln_v_b[0], w_s[0], b_s[0], g_out_a[0])
    out_b = _attention(z, q0, k0, v0, n_heads, n_kv, zc, g_out_b[0])
    o, w_down_b = _outproj(out_a, out_b, w_out_b, side=(w_down[0],))
    x_mix, h2 = _post(x2, o, g_post_mix[0], mods, g_pre_ffn[0])
    f = _ffn(h2, w_gu_b, conv_w[0], conv_b[0].reshape(1, 2 * ff), w_down_b)
    out = _final(x_mix, f, g_post_ffn[0], mods)
    return out[None]
```

```python
import functools

import numpy as np
import jax
import jax.numpy as jnp
from jax import lax
from jax.experimental import pallas as pl
from jax.experimental.pallas import tpu as pltpu

GRID_W = 64
HEAD_DIM = 128
ROPE_THETA = 10000.0
EPS = 1e-6
LOG2_E = 1.4426950408889634

LANES = 128
SUBLANES = 8
BF16_ROWS = 16
VMEM_LIMIT_BYTES = 56 * 1024 * 1024

ADALN_TN = 1024
ROW_TILE = 256
INPROJ_TM = 512
INPROJ_TN = 1024
GMLP_CHUNKS_PER_STEP = 4
ATTN_TQ = 256
ATTN_TK = 1024
ATTN_SCORE_AHEAD = 2
ATTN_VALUE_LAG = 2
OUTPROJ_TM = 512
OUTPROJ_TN = 1024
FFN_TM = 1024
FFN_TF = 256
FFN_ACT_ROWS = 256

F32 = jnp.float32
BF16 = jnp.bfloat16


def _params(n_axes):
    return pltpu.CompilerParams(dimension_semantics=("arbitrary",) * n_axes,
                                vmem_limit_bytes=VMEM_LIMIT_BYTES)


def _rms(x, g):
    return x * lax.rsqrt(jnp.mean(x * x, axis=-1, keepdims=True) + EPS) * g


def _adaln_kernel(c_ref, w_ref, b_ref, o_ref, s_ref):
    k_dim, tn = w_ref.shape
    nj = tn // LANES

    @pl.when(pl.program_id(0) == 0)
    def _():
        c = c_ref[...]
        s_ref[...] = c * jax.nn.sigmoid(c)

    def body(r, accs):
        rows = pl.ds(pl.multiple_of(r * SUBLANES, SUBLANES), SUBLANES)
        s0 = s_ref[0, rows, :]
        s1 = s_ref[1, rows, :]
        new = []
        for j in range(nj):
            w = w_ref[rows, j * LANES:(j + 1) * LANES]
            new.append(accs[2 * j] + w * s0)
            new.append(accs[2 * j + 1] + w * s1)
        return tuple(new)

    zero = jnp.zeros((SUBLANES, LANES), F32)
    accs = lax.fori_loop(0, k_dim // SUBLANES, body, (zero,) * (2 * nj), unroll=4)
    for j in range(nj):
        cols = slice(j * LANES, (j + 1) * LANES)
        o_ref[0:1, cols] = jnp.sum(accs[2 * j], axis=0, keepdims=True) + b_ref[:, cols]
        o_ref[1:2, cols] = jnp.sum(accs[2 * j + 1], axis=0, keepdims=True) + b_ref[:, cols]


def _adaln(cc, w_ada, b_ada):
    k_dim, n = w_ada.shape
    tn = min(ADALN_TN, n)
    c_rep = jnp.broadcast_to(cc[:, :, None], (2, k_dim, LANES))
    return pl.pallas_call(
        _adaln_kernel,
        grid=(n // tn,),
        in_specs=[pl.BlockSpec((2, k_dim, LANES), lambda j: (0, 0, 0)),
                  pl.BlockSpec((k_dim, tn), lambda j: (0, j)),
                  pl.BlockSpec((1, tn), lambda j: (0, j))],
        out_specs=pl.BlockSpec((2, tn), lambda j: (0, j)),
        out_shape=jax.ShapeDtypeStruct((2, n), F32),
        scratch_shapes=[pltpu.VMEM((2, k_dim, LANES), F32)],
        compiler_params=_params(1),
        name="adaln",
    )(c_rep, w_ada, b_ada.reshape(1, n))


def _prenorm_kernel(x_ref, g_ref, sh_ref, sc_ref, o_ref, *, row):
    y = _rms(x_ref[...], g_ref[...])
    o_ref[...] = (y * (1.0 + sc_ref[row:row + 1, :]) + sh_ref[row:row + 1, :]).astype(o_ref.dtype)


def _prenorm(x, g, mods, row, shift_idx, scale_idx):
    s, d = x.shape
    tm = min(ROW_TILE, s)
    return pl.pallas_call(
        functools.partial(_prenorm_kernel, row=row),
        grid=(s // tm,),
        in_specs=[pl.BlockSpec((tm, d), lambda i: (i, 0)),
                  pl.BlockSpec((1, d), lambda i: (0, 0)),
                  pl.BlockSpec((2, d), lambda i: (0, shift_idx)),
                  pl.BlockSpec((2, d), lambda i: (0, scale_idx))],
        out_specs=pl.BlockSpec((tm, d), lambda i: (i, 0)),
        out_shape=jax.ShapeDtypeStruct((s, d), BF16),
        compiler_params=_params(1),
        name="prenorm",
    )(x, g.reshape(1, d), mods, mods)


def _swap_rope_halves(a):
    lane = lax.broadcasted_iota(jnp.int32, a.shape, 1)
    return jnp.where((lane & 32) != 0, pltpu.roll(a, 32, 1), pltpu.roll(a, LANES - 32, 1))


def _head_epilogue(a, kind, cos, sin, gq, gk):
    if kind == "gelu":
        return 0.5 * a * (1.0 + lax.erf(a * (2.0 ** -0.5)))
    if kind == "v":
        return a
    a = _rms(a, gq if kind == "q" else gk)
    if kind != "kc":
        a = a * cos + _swap_rope_halves(a) * sin
    if kind == "q":
        a = a * (HEAD_DIM ** -0.5 * LOG2_E)
    return a


def _side_cast_specs(arrays, n_steps, n_inner):
    in_specs, out_specs, out_shapes, counts = [], [], [], []
    for a in arrays:
        rows, cols = a.shape
        rb = next(r for r in range(BF16_ROWS, rows + 1, BF16_ROWS)
                  if rows % r == 0 and rows // r <= n_steps)
        nb = rows // rb

        def index(j, i, nb=nb):
            return (jnp.minimum(j * n_inner + i, nb - 1), 0)

        in_specs.append(pl.BlockSpec((rb, cols), index))
        out_specs.append(pl.BlockSpec((rb, cols), index))
        out_shapes.append(jax.ShapeDtypeStruct((rows, cols), BF16))
        counts.append(nb)
    return in_specs, out_specs, out_shapes, tuple(counts)


def _side_cast(src_refs, dst_refs, counts):
    step = pl.program_id(0) * pl.num_programs(1) + pl.program_id(1)
    for src, dst, nb in zip(src_refs, dst_refs, counts):
        @pl.when(step < nb)
        def _(src=src, dst=dst):
            dst[...] = src[...].astype(dst.dtype)


def _inproj_kernel(h_ref, w_ref, cos_ref, sin_ref, gq_ref, gk_ref, *refs, plans, side_counts):
    n_side = len(side_counts)
    o_ref = refs[n_side]
    _side_cast(refs[:n_side], refs[n_side + 1:], side_counts)
    j = pl.program_id(0)
    for kinds, tiles in plans:
        cond = functools.reduce(jnp.logical_or, [j == t for t in tiles])

        @pl.when(cond)
        def _(kinds=kinds):
            acc = jnp.dot(h_ref[...], w_ref[...], preferred_element_type=F32)
            for c, kind in enumerate(kinds):
                cols = slice(c * LANES, (c + 1) * LANES)
                o_ref[:, cols] = _head_epilogue(acc[:, cols], kind, cos_ref[...], sin_ref[...],
                                                gq_ref[...], gk_ref[...]).astype(o_ref.dtype)


def _inproj(h, w, col0, kinds_per_chunk, cos_t, sin_t, gq, gk, side=()):
    s, d = h.shape
    n = LANES * len(kinds_per_chunk)
    tm = min(INPROJ_TM, s)
    tn = min(INPROJ_TN, n)
    assert n % tn == 0 and col0 % tn == 0 and s % tm == 0
    per_tile = tn // LANES
    by_kinds = {}
    for t in range(n // tn):
        by_kinds.setdefault(tuple(kinds_per_chunk[t * per_tile:(t + 1) * per_tile]), []).append(t)
    plans = tuple((k, tuple(v)) for k, v in by_kinds.items())
    j0 = col0 // tn
    grid = (n // tn, s // tm)
    side_in, side_out, side_shapes, side_counts = _side_cast_specs(side, grid[0] * grid[1], grid[1])
    return pl.pallas_call(
        functools.partial(_inproj_kernel, plans=plans, side_counts=side_counts),
        grid=grid,
        in_specs=[pl.BlockSpec((tm, d), lambda j, i: (i, 0)),
                  pl.BlockSpec((d, tn), lambda j, i: (0, j + j0)),
                  pl.BlockSpec((tm, LANES), lambda j, i: (i, 0)),
                  pl.BlockSpec((tm, LANES), lambda j, i: (i, 0)),
                  pl.BlockSpec((1, LANES), lambda j, i: (0, 0)),
                  pl.BlockSpec((1, LANES), lambda j, i: (0, 0))] + side_in,
        out_specs=[pl.BlockSpec((tm, tn), lambda j, i: (i, j))] + side_out,
        out_shape=[jax.ShapeDtypeStruct((s, n), BF16)] + side_shapes,
        compiler_params=_params(2),
        name="inproj",
    )(h, w, cos_t, sin_t, gq.reshape(1, LANES), gk.reshape(1, LANES), *side)


def _gmlp_kernel(z_ref, lng_ref, lnb_ref, ws_ref, bs_ref, go_ref, o_ref, oa_ref, *, chunk, groups):
    aw = groups * LANES
    for cidx in range(z_ref.shape[0] // chunk):
        rows = slice(cidx * chunk, (cidx + 1) * chunk)
        v = z_ref[rows, aw:].astype(F32)
        mu = jnp.mean(v, axis=-1, keepdims=True)
        vc = v - mu
        var = jnp.mean(vc * vc, axis=-1, keepdims=True)
        vn = (vc * lax.rsqrt(var + EPS) * lng_ref[...] + lnb_ref[...]).astype(BF16)
        for g in range(groups):
            cols = slice(g * LANES, (g + 1) * LANES)
            mixed = jnp.dot(ws_ref[g], vn[:, cols], preferred_element_type=F32) + bs_ref[g]
            oa_ref[:, cols] = z_ref[rows, cols].astype(F32) * mixed
        o_ref[rows, :] = _rms(oa_ref[...], go_ref[...]).astype(o_ref.dtype)


def _gmlp(z, n_cols_a, ln_g, ln_b, w_s, b_s, g_out):
    s = z.shape[0]
    groups, chunk, _ = w_s.shape
    aw = n_cols_a // 2
    assert aw == groups * LANES
    rows = chunk * min(GMLP_CHUNKS_PER_STEP, s // chunk)
    bs_rep = jnp.broadcast_to(b_s[:, :, None], (groups, chunk, LANES))
    return pl.pallas_call(
        functools.partial(_gmlp_kernel, chunk=chunk, groups=groups),
        grid=(s // rows,),
        in_specs=[pl.BlockSpec((rows, 2 * aw), lambda i: (i, 0)),
                  pl.BlockSpec((1, aw), lambda i: (0, 0)),
                  pl.BlockSpec((1, aw), lambda i: (0, 0)),
                  pl.BlockSpec((groups, chunk, chunk), lambda i: (0, 0, 0)),
                  pl.BlockSpec((groups, chunk, LANES), lambda i: (0, 0, 0)),
                  pl.BlockSpec((1, aw), lambda i: (0, 0))],
        out_specs=pl.BlockSpec((rows, aw), lambda i: (i, 0)),
        out_shape=jax.ShapeDtypeStruct((s, aw), BF16),
        scratch_shapes=[pltpu.VMEM((chunk, aw), F32)],
        compiler_params=_params(1),
        name="gmlp",
    )(z, ln_g.reshape(1, aw), ln_b.reshape(1, aw), w_s.astype(BF16), bs_rep, g_out.reshape(1, aw))


def _attn_kernel(q_ref, kc_ref, vc_ref, k_ref, v_ref, go_ref, o_ref, ob_ref, *, group, tk):
    kh = pl.program_id(1)
    n_kh = pl.num_programs(1)

    chunks = [(kc_ref, 0, kc_ref.shape[0])]
    chunks += [(k_ref, c * tk, tk) for c in range(k_ref.shape[0] // tk)]
    v_refs = {id(kc_ref): vc_ref, id(k_ref): v_ref}
    tiles = [(ci, g) for ci in range(len(chunks)) for g in range(group)]

    def scores(tile):
        ci, g = tile
        ref, lo, n = chunks[ci]
        q = q_ref[:, g * HEAD_DIM:(g + 1) * HEAD_DIM]
        return lax.dot_general(ref[lo:lo + n, :], q, (((1,), (1,)), ((), ())),
                               preferred_element_type=F32)

    stats = [None] * group
    accs = [None] * group
    v_t = {}

    def softmax(tile, s):
        g = tile[1]
        m_blk = jnp.max(s, axis=0, keepdims=True)
        if stats[g] is None:
            p = jnp.exp2(s - m_blk)
            stats[g] = (m_blk, jnp.sum(p, axis=0, keepdims=True))
            return tile, p.astype(BF16), None
        m_old, l_old = stats[g]
        m_new = jnp.maximum(m_old, m_blk)
        p = jnp.exp2(s - m_new)
        alpha = jnp.exp2(m_old - m_new)
        stats[g] = (m_new, alpha * l_old + jnp.sum(p, axis=0, keepdims=True))
        return tile, p.astype(BF16), alpha

    def values(pending):
        (ci, g), p, alpha = pending
        if ci not in v_t:
            ref, lo, n = chunks[ci]
            v_t[ci] = v_refs[id(ref)][lo:lo + n, :].T
        pv = jnp.dot(v_t[ci], p, preferred_element_type=F32)
        accs[g] = pv if alpha is None else alpha * accs[g] + pv

    ready = [scores(tiles[t]) for t in range(min(ATTN_SCORE_AHEAD, len(tiles)))]
    pending = []
    for t, tile in enumerate(tiles):
        if t + ATTN_SCORE_AHEAD < len(tiles):
            ready.append(scores(tiles[t + ATTN_SCORE_AHEAD]))
        pending.append(softmax(tile, ready.pop(0)))
        if len(pending) > ATTN_VALUE_LAG:
            values(pending.pop(0))
    for item in pending:
        values(item)

    for g in range(group):
        ob_ref[kh * group + g] = (accs[g] / stats[g][1]).T

    @pl.when(kh == n_kh - 1)
    def _():
        n_heads = ob_ref.shape[0]
        ssq = jnp.zeros((q_ref.shape[0], 1), F32)
        for h in range(n_heads):
            o = ob_ref[h]
            ssq = ssq + jnp.sum(o * o, axis=-1, keepdims=True)
        inv = lax.rsqrt(ssq / (n_heads * HEAD_DIM) + EPS)
        for h in range(n_heads):
            cols = slice(h * HEAD_DIM, (h + 1) * HEAD_DIM)
            o_ref[:, cols] = (ob_ref[h] * inv * go_ref[:, cols]).astype(o_ref.dtype)


def _attention(z, q_col0, k_col0, v_col0, n_heads, n_kv, zc, g_out):
    s = z.shape[0]
    c = zc.shape[0]
    group = n_heads // n_kv
    gw = group * HEAD_DIM
    bw = n_heads * HEAD_DIM
    tq = min(ATTN_TQ, s)
    tk = min(ATTN_TK, s)
    assert q_col0 % gw == 0 and k_col0 % HEAD_DIM == 0 and v_col0 % HEAD_DIM == 0
    qb, kb, vb = q_col0 // gw, k_col0 // HEAD_DIM, v_col0 // HEAD_DIM
    return pl.pallas_call(
        functools.partial(_attn_kernel, group=group, tk=tk),
        grid=(s // tq, n_kv),
        in_specs=[pl.BlockSpec((tq, gw), lambda i, kh: (i, qb + kh)),
                  pl.BlockSpec((c, HEAD_DIM), lambda i, kh: (0, kh)),
                  pl.BlockSpec((c, HEAD_DIM), lambda i, kh: (0, n_kv + kh)),
                  pl.BlockSpec((s, HEAD_DIM), lambda i, kh: (0, kb + kh)),
                  pl.BlockSpec((s, HEAD_DIM), lambda i, kh: (0, vb + kh)),
                  pl.BlockSpec((1, bw), lambda i, kh: (0, 0))],
        out_specs=pl.BlockSpec((tq, bw), lambda i, kh: (i, 0)),
        out_shape=jax.ShapeDtypeStruct((s, bw), BF16),
        scratch_shapes=[pltpu.VMEM((n_heads, tq, HEAD_DIM), F32)],
        compiler_params=_params(2),
        name="attn",
    )(z, zc, zc, z, z, g_out.reshape(1, bw))


def _outproj_kernel(a_ref, b_ref, wa_ref, wb_ref, *refs, side_counts):
    n_side = len(side_counts)
    o_ref = refs[n_side]
    _side_cast(refs[:n_side], refs[n_side + 1:], side_counts)
    o_ref[...] = (jnp.dot(a_ref[...], wa_ref[...], preferred_element_type=F32)
                  + jnp.dot(b_ref[...], wb_ref[...], preferred_element_type=F32))


def _outproj(a, b, w, side=()):
    s, aw = a.shape
    bw = b.shape[1]
    n = w.shape[1]
    tm = min(OUTPROJ_TM, s)
    tn = min(OUTPROJ_TN, n)
    assert aw == bw
    grid = (n // tn, s // tm)
    side_in, side_out, side_shapes, side_counts = _side_cast_specs(side, grid[0] * grid[1], grid[1])
    return pl.pallas_call(
        functools.partial(_outproj_kernel, side_counts=side_counts),
        grid=grid,
        in_specs=[pl.BlockSpec((tm, aw), lambda j, i: (i, 0)),
                  pl.BlockSpec((tm, bw), lambda j, i: (i, 0)),
                  pl.BlockSpec((aw, tn), lambda j, i: (0, j)),
                  pl.BlockSpec((bw, tn), lambda j, i: (1, j))] + side_in,
        out_specs=[pl.BlockSpec((tm, tn), lambda j, i: (i, j))] + side_out,
        out_shape=[jax.ShapeDtypeStruct((s, n), F32)] + side_shapes,
        compiler_params=_params(2),
        name="outproj",
    )(a, b, w, w, *side)


def _post_kernel(x_ref, o_ref, gpost_ref, gate_ref, gpre_ref, sh_ref, sc_ref, xm_ref, h_ref):
    xm = x_ref[...] + gate_ref[0:1, :] * _rms(o_ref[...], gpost_ref[...])
    xm_ref[...] = xm
    h = _rms(xm, gpre_ref[...])
    h_ref[...] = (h * (1.0 + sc_ref[0:1, :]) + sh_ref[0:1, :]).astype(h_ref.dtype)


def _post(x, o, g_post, mods, g_pre):
    s, d = x.shape
    tm = min(ROW_TILE, s)
    row = pl.BlockSpec((tm, d), lambda i: (i, 0))
    vec = pl.BlockSpec((1, d), lambda i: (0, 0))

    def mod(idx):
        return pl.BlockSpec((2, d), lambda i: (0, idx))

    return pl.pallas_call(
        _post_kernel,
        grid=(s // tm,),
        in_specs=[row, row, vec, mod(2), vec, mod(3), mod(4)],
        out_specs=[row, row],
        out_shape=[jax.ShapeDtypeStruct((s, d), F32), jax.ShapeDtypeStruct((s, d), BF16)],
        compiler_params=_params(1),
        name="post",
    )(x, o, g_post.reshape(1, d), mods, g_pre.reshape(1, d), mods, mods)


def _ffn_kernel(h_hbm, wg_ref, wu_ref, cwg_ref, cwu_ref, cbg_ref, cbu_ref, wd_ref,
                o_ref, lhs_ref, r0_ref, r1_ref, act_ref, sem, *, nf, tm, halo):
    i = pl.program_id(0)
    n_i = pl.num_programs(0)
    f = pl.program_id(1)
    r_refs = (r0_ref, r1_ref)

    def lhs_copies(tile, action):
        slot = tile % 2
        row0 = pl.multiple_of(tile * tm, tm)
        getattr(pltpu.make_async_copy(h_hbm.at[pl.ds(row0, tm)],
                                      lhs_ref.at[slot, pl.ds(halo, tm)], sem.at[slot, 0]), action)()

        @pl.when(tile > 0)
        def _():
            getattr(pltpu.make_async_copy(h_hbm.at[pl.ds(row0 - halo, halo)],
                                          lhs_ref.at[slot, pl.ds(0, halo)], sem.at[slot, 1]), action)()

        @pl.when(tile < n_i - 1)
        def _():
            getattr(pltpu.make_async_copy(h_hbm.at[pl.ds(row0 + tm, halo)],
                                          lhs_ref.at[slot, pl.ds(halo + tm, halo)], sem.at[slot, 2]), action)()

    def up_proj(r_ref, which):
        w_ref = (wg_ref, wu_ref)[which]
        r_ref[which] = jnp.dot(lhs_ref[i % 2], w_ref[...], preferred_element_type=F32)

    def conv(win, cw_ref, cb_ref):
        n = win.shape[0]
        prev = pltpu.roll(win, 1, 0)[SUBLANES:n - SUBLANES, :]
        nxt = pltpu.roll(win, n - 1, 0)[SUBLANES:n - SUBLANES, :]
        return (prev * cw_ref[0:1, :] + win[SUBLANES:n - SUBLANES, :] * cw_ref[1:2, :]
                + nxt * cw_ref[2:3, :] + cb_ref[...])

    def activation(r_ref):
        rb = min(FFN_ACT_ROWS, tm)
        for b in range(tm // rb):
            lo = halo + b * rb - SUBLANES
            gate = conv(r_ref[0, lo:lo + rb + 2 * SUBLANES, :], cwg_ref, cbg_ref)
            up = conv(r_ref[1, lo:lo + rb + 2 * SUBLANES, :], cwu_ref, cbu_ref)
            act_ref[b * rb:(b + 1) * rb, :] = (gate * jax.nn.sigmoid(gate) * up).astype(BF16)

    def down_proj():
        o_ref[...] += jnp.dot(act_ref[...], wd_ref[...], preferred_element_type=F32)

    @pl.when(jnp.logical_and(f == 0, i == 0))
    def _():
        lhs_copies(i, "start")

    @pl.when(jnp.logical_and(f == 1, i < n_i - 1))
    def _():
        lhs_copies(i + 1, "start")

    @pl.when(f == 0)
    def _():
        lhs_copies(i, "wait")
        zeros = jnp.zeros((halo, lhs_ref.shape[2]), BF16)

        @pl.when(i == 0)
        def _():
            lhs_ref[i % 2, 0:halo, :] = zeros

        @pl.when(i == n_i - 1)
        def _():
            lhs_ref[i % 2, halo + tm:, :] = zeros

        o_ref[...] = jnp.zeros(o_ref.shape, F32)
        up_proj(r0_ref, 0)
        up_proj(r0_ref, 1)

    for parity in (0, 1):
        @pl.when(jnp.logical_and(jnp.logical_and(f > 0, f < nf), f % 2 == parity))
        def _(parity=parity):
            activation(r_refs[1 - parity])
            up_proj(r_refs[parity], 0)
            down_proj()
            up_proj(r_refs[parity], 1)

    @pl.when(f == nf)
    def _():
        activation(r_refs[(nf - 1) % 2])
        down_proj()


def _ffn(h, w_gu, conv_w, conv_b, w_down):
    s, d = h.shape
    ff = w_down.shape[0]
    tm = min(FFN_TM, s)
    tf = min(FFN_TF, ff)
    assert ff % tf == 0 and s % tm == 0 and tm % BF16_ROWS == 0
    nf = ff // tf
    single = pl.Buffered(1)

    def cur(f):
        return jnp.minimum(f, nf - 1)

    def prv(f):
        return jnp.maximum(f - 1, 0)

    return pl.pallas_call(
        functools.partial(_ffn_kernel, nf=nf, tm=tm, halo=BF16_ROWS),
        grid=(s // tm, nf + 1),
        in_specs=[pl.BlockSpec(memory_space=pl.ANY),
                  pl.BlockSpec((d, tf), lambda i, f: (0, cur(f))),
                  pl.BlockSpec((d, tf), lambda i, f: (0, nf + cur(f))),
                  pl.BlockSpec((3, tf), lambda i, f: (0, prv(f))),
                  pl.BlockSpec((3, tf), lambda i, f: (0, nf + prv(f))),
                  pl.BlockSpec((1, tf), lambda i, f: (0, prv(f))),
                  pl.BlockSpec((1, tf), lambda i, f: (0, nf + prv(f))),
                  pl.BlockSpec((tf, d), lambda i, f: (prv(f), 0))],
        out_specs=pl.BlockSpec((tm, d), lambda i, f: (i, 0), pipeline_mode=single),
        out_shape=jax.ShapeDtypeStruct((s, d), F32),
        scratch_shapes=[pltpu.VMEM((2, tm + 2 * BF16_ROWS, d), BF16),
                        pltpu.VMEM((2, tm + 2 * BF16_ROWS, tf), F32),
                        pltpu.VMEM((2, tm + 2 * BF16_ROWS, tf), F32),
                        pltpu.VMEM((tm, tf), BF16),
                        pltpu.SemaphoreType.DMA((2, 3))],
        compiler_params=_params(2),
        name="ffn",
    )(h, w_gu, w_gu, conv_w, conv_w, conv_b, conv_b, w_down)


def _final_kernel(xm_ref, f_ref, g_ref, gate_ref, o_ref):
    o_ref[...] = xm_ref[...] + gate_ref[0:1, :] * _rms(f_ref[...], g_ref[...])


def _final(xm, f, g_post, mods):
    s, d = xm.shape
    tm = min(ROW_TILE, s)
    row = pl.BlockSpec((tm, d), lambda i: (i, 0))
    return pl.pallas_call(
        _final_kernel,
        grid=(s // tm,),
        in_specs=[row, row, pl.BlockSpec((1, d), lambda i: (0, 0)),
                  pl.BlockSpec((2, d), lambda i: (0, 5))],
        out_specs=row,
        out_shape=jax.ShapeDtypeStruct((s, d), F32),
        compiler_params=_params(1),
        name="final",
    )(xm, f, g_post.reshape(1, d), mods)


def _rope_tables(s):
    quarter = HEAD_DIM // 4
    pos = np.arange(s)
    freqs = ROPE_THETA ** (-np.arange(quarter, dtype=np.float64) / quarter)
    ang_r = (pos // GRID_W)[:, None] * freqs[None, :]
    ang_c = (pos % GRID_W)[:, None] * freqs[None, :]
    cos_t = np.concatenate([np.cos(ang_r), np.cos(ang_r), np.cos(ang_c), np.cos(ang_c)], axis=1)
    sin_t = np.concatenate([-np.sin(ang_r), np.sin(ang_r), -np.sin(ang_c), np.sin(ang_c)], axis=1)
    return jnp.asarray(cos_t, F32), jnp.asarray(sin_t, F32)


def kernel(x, c, ctx, c_ctx, w_ada, b_ada, g_pre_mix, g_post_mix, g_pre_ffn, g_post_ffn, w_in, ln_v_g,
           ln_v_b, w_s, b_s, g_q, g_k, g_out_a, g_out_b, w_out, w_up, conv_w, conv_b, w_down):
    bsz, s, d = x.shape
    assert bsz == 1 and w_in.shape[0] == 1, "single batch element, single layer"
    aw = ln_v_g.shape[-1]
    bw = g_out_b.shape[-1]
    in_cols = w_in.shape[-1]
    kvw = (in_cols - 2 * aw - bw) // 2
    n_heads, n_kv = bw // HEAD_DIM, kvw // HEAD_DIM
    q0, k0, v0 = 2 * aw, 2 * aw + bw, 2 * aw + bw + kvw
    ff = w_down.shape[1]

    x2 = x[0]
    ctx2 = ctx[0]
    w_in_b = w_in[0].astype(BF16)

    mods = _adaln(jnp.concatenate([c, c_ctx[None, :]], axis=0), w_ada[0], b_ada[0])

    cos_t, sin_t = _rope_tables(s)
    h = _prenorm(x2, g_pre_mix[0], mods, 0, 0, 1)
    hc = _prenorm(ctx2, g_pre_mix[0], mods, 1, 0, 1)

    kinds = (["gelu"] * (2 * aw // LANES) + ["q"] * n_heads + ["k"] * n_kv + ["v"] * n_kv)
    z, w_gu_b, w_out_b = _inproj(h, w_in_b, 0, kinds, cos_t, sin_t, g_q[0], g_k[0],
                                 side=(w_up[0], w_out[0]))
    zc, = _inproj(hc, w_in_b, k0, ["kc"] * n_kv + ["v"] * n_kv, cos_t, sin_t, g_q[0], g_k[0])

    out_a = _gmlp(z, 2 * aw, ln_v_g[0], ln_v_b[0], w_s[0], b_s[0], g_out_a[0])
    out_b = _attention(z, q0, k0, v0, n_heads, n_kv, zc, g_out_b[0])
    o, w_down_b = _outproj(out_a, out_b, w_out_b, side=(w_down[0],))
    x_mix, h2 = _post(x2, o, g_post_mix[0], mods, g_pre_ffn[0])
    f = _ffn(h2, w_gu_b, conv_w[0], conv_b[0].reshape(1, 2 * ff), w_down_b)
    out = _final(x_mix, f, g_post_ffn[0], mods)
    return out[None]
```
